```python
import jax, jax.numpy as jnp
from jax import lax
import numpy as np

D_MODEL = 2048
BATCH = 1
SEQ = 8192
DEPTH = 1

HEAD_DIM = 128
NSA_WIDTH = D_MODEL // 2
NSA_HEADS = NSA_WIDTH // HEAD_DIM
NSA_GROUPS = 2
HEADS_PER_GROUP = NSA_HEADS // NSA_GROUPS
KV_WIDTH = NSA_GROUPS * HEAD_DIM
NSA_BRANCHES = 3
CMP_BLOCK = 32
CMP_STRIDE = 16
SEL_BLOCK = 64
SEL_TOPK = 16
N_LOCAL_SEL = 2
WINDOW = 512
Q_BLOCK = 128
ROPE_THETA = 10000.0
ATTN_SCALE = HEAD_DIM ** -0.5
FORCE_BONUS = 1.0e4
NEG = -1.0e30

RWKV_WIDTH = D_MODEL // 2
RWKV_HEAD = 64
RWKV_HEADS = RWKV_WIDTH // RWKV_HEAD
DECAY_LORA = 96
ICLR_LORA = 96
GATE_LORA = 256
RWKV_SHIFT_WIDTH = 3 * RWKV_WIDTH + DECAY_LORA + ICLR_LORA + GATE_LORA
GN_EPS = 64e-5

D_FF = 5632
CONV_WIDTH = 3
EPS = 1e-6

IN_SIZES = (NSA_WIDTH, KV_WIDTH, KV_WIDTH, KV_WIDTH, KV_WIDTH, KV_WIDTH, KV_WIDTH, NSA_HEADS * NSA_BRANCHES, RWKV_SHIFT_WIDTH, 2 * D_MODEL)
IN_WIDTH = NSA_WIDTH + 6 * KV_WIDTH + NSA_HEADS * NSA_BRANCHES + RWKV_SHIFT_WIDTH + 2 * D_MODEL
RWKV_SIZES = (RWKV_WIDTH, RWKV_WIDTH, RWKV_WIDTH, DECAY_LORA, ICLR_LORA, GATE_LORA)

kernel_name = 'nsa_rwkv7_gated_hybrid_block'


def _split(t, sizes):
    return jnp.split(t, np.cumsum(sizes)[:-1].tolist(), axis=-1)


def rms_norm(x, g):
    xf = x.astype(jnp.float32)
    y = xf * lax.rsqrt(jnp.mean(xf * xf, axis=-1, keepdims=True) + EPS)
    return (y * g.astype(jnp.float32)).astype(x.dtype)


def rope(t):
    seq = t.shape[1]
    half = HEAD_DIM // 2
    inv = ROPE_THETA ** (-jnp.arange(half, dtype=jnp.float32) / half)
    ang = jnp.arange(seq, dtype=jnp.float32)[:, None] * inv[None, :]
    cos = jnp.cos(ang)[None, :, None, :]
    sin = jnp.sin(ang)[None, :, None, :]
    tf = t.astype(jnp.float32)
    t1, t2 = tf[..., :half], tf[..., half:]
    return jnp.concatenate([t1 * cos - t2 * sin, t1 * sin + t2 * cos], axis=-1).astype(t.dtype)


def masked_softmax(s, mask, axis=-1):
    p = jax.nn.softmax(jnp.where(mask, s, NEG), axis=axis)
    return jnp.where(mask, p, 0.0)


def compress(t, pe, w1, w2):
    b, seq = t.shape[0], t.shape[1]
    n_cmp = (seq - CMP_BLOCK) // CMP_STRIDE + 1
    idx = (jnp.arange(n_cmp) * CMP_STRIDE)[:, None] + jnp.arange(CMP_BLOCK)[None, :]
    blk = t[:, idx] + pe[None, None, :, None, :]
    blk = jnp.swapaxes(blk, 2, 3).reshape(b, n_cmp, NSA_GROUPS, CMP_BLOCK * HEAD_DIM)
    return jax.nn.gelu(blk @ w1) @ w2


def nsa_attention(q, k_cmp, v_cmp, k_slc, v_slc, k_win, v_win, branch_gate, ck_pe, ck_w1, ck_w2, cv_pe, cv_w1, cv_w2):
    b, seq, _ = q.shape
    G, J, dh = NSA_GROUPS, HEADS_PER_GROUP, HEAD_DIM
    nqb = seq // Q_BLOCK
    t = jnp.arange(seq)
    qg = rope(q.reshape(b, seq, NSA_HEADS, dh)).reshape(b, seq, G, J, dh)
    kv = lambda z: z.reshape(b, seq, G, dh)

    kc = compress(rope(kv(k_cmp)), ck_pe, ck_w1, ck_w2)
    vc = compress(kv(v_cmp), cv_pe, cv_w1, cv_w2)
    n_cmp = kc.shape[1]
    c_start = jnp.arange(n_cmp) * CMP_STRIDE
    cmask = (c_start + CMP_BLOCK - 1)[None, :] <= t[:, None]
    s_cmp = jnp.einsum('bsgjd,bcgd->bgjsc', qg, kc).astype(jnp.float32) * ATTN_SCALE
    p_cmp = masked_softmax(s_cmp, cmask)
    o_cmp = jnp.einsum('bgjsc,bcgd->bsgjd', p_cmp.astype(vc.dtype), vc)

    n_sel = seq // SEL_BLOCK
    top = min(SEL_TOPK, n_sel)
    s_start = jnp.arange(n_sel) * SEL_BLOCK
    overlap = jnp.clip(jnp.minimum(c_start[:, None] + CMP_BLOCK, s_start[None, :] + SEL_BLOCK)
                       - jnp.maximum(c_start[:, None], s_start[None, :]), 0).astype(jnp.float32) / CMP_BLOCK
    imp = jnp.einsum('bgjsc,cn->bgsn', p_cmp, overlap)
    cur = (t // SEL_BLOCK)[:, None]
    j_idx = jnp.arange(n_sel)[None, :]
    forced = (j_idx == 0) | ((j_idx <= cur) & (j_idx > cur - N_LOCAL_SEL))
    smask = s_start[None, :] <= t[:, None]
    imp = jnp.where(smask, imp + FORCE_BONUS * forced.astype(jnp.float32), NEG)
    vals, sel_idx = lax.top_k(imp, top)
    sel_ok = vals > 0.5 * NEG

    to_blocks = lambda z: jnp.transpose(z.reshape(b, n_sel, SEL_BLOCK, G, dh), (0, 3, 1, 2, 4))
    k_blocks = to_blocks(rope(kv(k_slc)))
    v_blocks = to_blocks(kv(v_slc))
    bi = jnp.arange(b)[:, None, None, None]
    gi = jnp.arange(G)[None, :, None, None]
    q_blk = jnp.moveaxis(qg.reshape(b, nqb, Q_BLOCK, G, J, dh), 1, 0)
    i_blk = jnp.moveaxis(sel_idx.reshape(b, G, nqb, Q_BLOCK, top), 2, 0)
    ok_blk = jnp.moveaxis(sel_ok.reshape(b, G, nqb, Q_BLOCK, top), 2, 0)

    def attend_selected(args):
        qb, ib, okb, n = args
        kg = k_blocks[bi, gi, ib]
        vg = v_blocks[bi, gi, ib]
        s = jnp.einsum('bqgjd,bgqkld->bgjqkl', qb, kg).astype(jnp.float32) * ATTN_SCALE
        key_pos = ib[..., None] * SEL_BLOCK + jnp.arange(SEL_BLOCK)
        q_pos = n * Q_BLOCK + jnp.arange(Q_BLOCK)
        mask = (key_pos <= q_pos[None, None, :, None, None]) & okb[..., None]
        p = masked_softmax(s, mask[:, :, None], axis=(-2, -1))
        return jnp.einsum('bgjqkl,bgqkld->bqgjd', p.astype(vg.dtype), vg)

    o_slc = lax.map(attend_selected, (q_blk, i_blk, ok_blk, jnp.arange(nqb)))
    o_slc = jnp.moveaxis(o_slc, 0, 1).reshape(b, seq, G, J, dh)

    n_prev = WINDOW // Q_BLOCK
    band_len = (n_prev + 1) * Q_BLOCK

    def band(z):
        zp = jnp.pad(jnp.transpose(z, (0, 2, 1, 3)), ((0, 0), (0, 0), (WINDOW, 0), (0, 0)))
        zb = zp.reshape(b, G, nqb + n_prev, Q_BLOCK, dh)
        return jnp.concatenate([zb[:, :, i:i + nqb] for i in range(n_prev + 1)], axis=3)

    kw = band(rope(kv(k_win)))
    vw = band(kv(v_win))
    qw = qg.reshape(b, nqb, Q_BLOCK, G, J, dh)
    s_win = jnp.einsum('bnqgjd,bgnkd->bgjnqk', qw, kw).astype(jnp.float32) * ATTN_SCALE
    q_pos = (jnp.arange(nqb) * Q_BLOCK)[:, None] + jnp.arange(Q_BLOCK)[None, :]
    k_pos = (jnp.arange(nqb) * Q_BLOCK - WINDOW)[:, None] + jnp.arange(band_len)[None, :]
    kp, qp = k_pos[:, None, :], q_pos[:, :, None]
    wmask = (kp <= qp) & (kp > qp - WINDOW) & (kp >= 0)
    p_win = masked_softmax(s_win, wmask)
    o_win = jnp.einsum('bgjnqk,bgnkd->bnqgjd', p_win.astype(vw.dtype), vw).reshape(b, seq, G, J, dh)

    g = jax.nn.sigmoid(branch_gate).reshape(b, seq, G, J, NSA_BRANCHES)
    o = g[..., 0:1] * o_cmp + g[..., 1:2] * o_slc + g[..., 2:3] * o_win
    return o.reshape(b, seq, NSA_WIDTH)


def rwkv7_time_mix(z, mu, w0, w_up, a0, a_up, g_up, k_k, k_a, r_k, ln_w, ln_b):
    b, seq, _ = z.shape
    H, N = RWKV_HEADS, RWKV_HEAD
    f32 = jnp.float32
    z_prev = jnp.pad(z, ((0, 0), (1, 0), (0, 0)))[:, :-1]
    z = z + (z_prev - z) * mu
    r, k, v, wd, ad, gd = _split(z, RWKV_SIZES)
    w = -jax.nn.softplus(-(w0 + jnp.tanh(wd) @ w_up).astype(f32)) - 0.5
    decay = jnp.exp(-jnp.exp(w))
    a = jax.nn.sigmoid(a0 + ad @ a_up)
    g = jax.nn.sigmoid(gd) @ g_up
    heads = lambda u: u.astype(f32).reshape(b, seq, H, N)
    kk = heads(k * k_k)
    kk = kk * lax.rsqrt(jnp.maximum(jnp.sum(kk * kk, axis=-1, keepdims=True), 1e-24))
    k = k * (1.0 + (a - 1.0) * k_a)
    rh, kh, vh, wh, ah = heads(r), heads(k), heads(v), heads(decay), heads(a)
    a_vec = -kk
    b_vec = kk * ah

    def step(state, inp):
        r_t, w_t, k_t, v_t, a_t, b_t = inp
        sa = jnp.einsum('bhvk,bhk->bhv', state, a_t)
        state = state * w_t[:, :, None, :] + sa[..., None] * b_t[:, :, None, :] + v_t[..., None] * k_t[:, :, None, :]
        return state, jnp.einsum('bhvk,bhk->bhv', state, r_t)

    xs = tuple(jnp.moveaxis(u, 1, 0) for u in (rh, wh, kh, vh, a_vec, b_vec))
    _, out = lax.scan(step, jnp.zeros((b, H, N, N), f32), xs)
    out = jnp.moveaxis(out, 0, 1)
    mean = jnp.mean(out, axis=-1, keepdims=True)
    var = jnp.mean(jnp.square(out - mean), axis=-1, keepdims=True)
    out = (out - mean) * lax.rsqrt(var + GN_EPS) * ln_w.astype(f32).reshape(H, N) + ln_b.astype(f32).reshape(H, N)
    out = out + jnp.sum(rh * kh * r_k.astype(f32), axis=-1, keepdims=True) * vh
    return (out.reshape(b, seq, RWKV_WIDTH) * g.astype(f32)).astype(z.dtype)


def conv_glu_ffn(h, w_up, conv_w, conv_b, w_down):
    seq = h.shape[1]
    u = h @ w_up
    up = jnp.pad(u, ((0, 0), (CONV_WIDTH - 1, 0), (0, 0)))
    uc = conv_b + up[:, 0:seq] * conv_w[0]
    for i in range(1, CONV_WIDTH):
        uc = uc + up[:, i:i + seq] * conv_w[i]
    gate, val = jnp.split(uc, 2, axis=-1)
    return (jax.nn.silu(gate) * val) @ w_down


def setup_inputs(seed: int = 0) -> dict:
    key = jax.random.key(seed)
    ks = jax.random.split(key, 32)
    L = DEPTH
    nrm = lambda k, shape, scale: jax.random.normal(k, shape, jnp.float32) * scale
    uni = lambda k, shape: jax.random.uniform(k, shape, jnp.float32)
    return {
        'x': nrm(ks[0], (BATCH, SEQ, D_MODEL), 1.0),
        'norm1_g': 1.0 + nrm(ks[1], (L, D_MODEL), 0.02),
        'w_in': nrm(ks[2], (L, D_MODEL, IN_WIDTH), D_MODEL ** -0.5),
        'cmp_k_pe': nrm(ks[3], (L, CMP_BLOCK, HEAD_DIM), 0.02),
        'cmp_k_w1': nrm(ks[4], (L, CMP_BLOCK * HEAD_DIM, HEAD_DIM), (CMP_BLOCK * HEAD_DIM) ** -0.5),
        'cmp_k_w2': nrm(ks[5], (L, HEAD_DIM, HEAD_DIM), HEAD_DIM ** -0.5),
        'cmp_v_pe': nrm(ks[6], (L, CMP_BLOCK, HEAD_DIM), 0.02),
        'cmp_v_w1': nrm(ks[7], (L, CMP_BLOCK * HEAD_DIM, HEAD_DIM), (CMP_BLOCK * HEAD_DIM) ** -0.5),
        'cmp_v_w2': nrm(ks[8], (L, HEAD_DIM, HEAD_DIM), HEAD_DIM ** -0.5),
        'rwkv_mu': uni(ks[9], (L, RWKV_SHIFT_WIDTH)),
        'rwkv_w0': -5.0 + 4.0 * uni(ks[10], (L, RWKV_WIDTH)),
        'rwkv_w_up': nrm(ks[11], (L, DECAY_LORA, RWKV_WIDTH), 0.1),
        'rwkv_a0': nrm(ks[12], (L, RWKV_WIDTH), 0.5),
        'rwkv_a_up': nrm(ks[13], (L, ICLR_LORA, RWKV_WIDTH), 0.1),
        'rwkv_g_up': nrm(ks[14], (L, GATE_LORA, RWKV_WIDTH), GATE_LORA ** -0.5),
        'rwkv_k_k': 0.85 + nrm(ks[15], (L, RWKV_WIDTH), 0.02),
        'rwkv_k_a': 1.0 + nrm(ks[16], (L, RWKV_WIDTH), 0.02),
        'rwkv_r_k': nrm(ks[17], (L, RWKV_HEADS, RWKV_HEAD), 0.1),
        'rwkv_ln_w': 1.0 + nrm(ks[18], (L, RWKV_WIDTH), 0.02),
        'rwkv_ln_b': nrm(ks[19], (L, RWKV_WIDTH), 0.02),
        'proj_nsa': nrm(ks[20], (L, NSA_WIDTH, D_MODEL), NSA_WIDTH ** -0.5),
        'proj_rwkv': nrm(ks[21], (L, RWKV_WIDTH, D_MODEL), RWKV_WIDTH ** -0.5),
        'w_out': nrm(ks[22], (L, D_MODEL, D_MODEL), D_MODEL ** -0.5),
        'norm2_g': 1.0 + nrm(ks[23], (L, D_MODEL), 0.02),
        'ffn_up': nrm(ks[24], (L, D_MODEL, 2 * D_FF), D_MODEL ** -0.5),
        'ffn_conv_w': nrm(ks[25], (L, CONV_WIDTH, 2 * D_FF), CONV_WIDTH ** -0.5),
        'ffn_conv_b': nrm(ks[26], (L, 2 * D_FF), 0.02),
        'ffn_down': nrm(ks[27], (L, D_FF, D_MODEL), D_FF ** -0.5),
        'final_g': 1.0 + nrm(ks[28], (D_MODEL,), 0.02),
    }


def reference(x, norm1_g, w_in, cmp_k_pe, cmp_k_w1, cmp_k_w2, cmp_v_pe, cmp_v_w1, cmp_v_w2,
              rwkv_mu, rwkv_w0, rwkv_w_up, rwkv_a0, rwkv_a_up, rwkv_g_up, rwkv_k_k, rwkv_k_a, rwkv_r_k,
              rwkv_ln_w, rwkv_ln_b, proj_nsa, proj_rwkv, w_out, norm2_g, ffn_up, ffn_conv_w, ffn_conv_b,
              ffn_down, final_g):
    for l in range(DEPTH):
        h = rms_norm(x, norm1_g[l])
        proj = h @ w_in[l]
        q, k_c, v_c, k_s, v_s, k_w, v_w, nsa_gate, z_rwkv, merge_gate = _split(proj, IN_SIZES)
        o_a = nsa_attention(q, k_c, v_c, k_s, v_s, k_w, v_w, nsa_gate,
                            cmp_k_pe[l], cmp_k_w1[l], cmp_k_w2[l], cmp_v_pe[l], cmp_v_w1[l], cmp_v_w2[l])
        o_b = rwkv7_time_mix(z_rwkv, rwkv_mu[l], rwkv_w0[l], rwkv_w_up[l], rwkv_a0[l], rwkv_a_up[l],
                             rwkv_g_up[l], rwkv_k_k[l], rwkv_k_a[l], rwkv_r_k[l], rwkv_ln_w[l], rwkv_ln_b[l])
        gate_a, gate_b = jnp.split(jax.nn.sigmoid(merge_gate), 2, axis=-1)
        mixed = gate_a * (o_a @ proj_nsa[l]) + gate_b * (o_b @ proj_rwkv[l])
        x = x + mixed @ w_out[l]
        h = rms_norm(x, norm2_g[l])
        x = x + conv_glu_ffn(h, ffn_up[l], ffn_conv_w[l], ffn_conv_b[l], ffn_down[l])
    return rms_norm(x, final_g)
```

```python
import functools

import jax
import jax.numpy as jnp
from jax import lax
from jax.experimental import pallas as pl
from jax.experimental.pallas import tpu as pltpu

F32 = jnp.float32
BF16 = jnp.bfloat16
HI = lax.Precision.HIGHEST

LANES = 128
D_MODEL = 2048
HEAD_DIM = 128
NSA_WIDTH = D_MODEL // 2
NSA_HEADS = NSA_WIDTH // HEAD_DIM
NSA_GROUPS = 2
HPG = NSA_HEADS // NSA_GROUPS
KV_WIDTH = NSA_GROUPS * HEAD_DIM
NSA_BRANCHES = 3
CMP_BLOCK = 32
CMP_STRIDE = 16
SEL_BLOCK = 64
SEL_TOPK = 16
N_LOCAL_SEL = 2
WINDOW = 512
Q_BLOCK = 128
ROPE_THETA = 10000.0
ATTN_SCALE = HEAD_DIM ** -0.5
FORCE_BONUS = 1.0e4
NEG = -1.0e30
PICKED = -3.0e38

RWKV_WIDTH = D_MODEL // 2
RWKV_HEAD = 64
DECAY_LORA = 96
ICLR_LORA = 96
GATE_LORA = 256
LORA_PAD = 128
RWKV_Z = 3 * RWKV_WIDTH + 2 * LORA_PAD + GATE_LORA
GN_EPS = 64e-5
SCAN_CHUNK = 64

D_FF = 5632
CONV_WIDTH = 3
EPS = 1e-6

QKV_WIDTH = NSA_WIDTH + 6 * KV_WIDTH
QKV_BLOCKS = QKV_WIDTH // LANES
ROPE_BLOCKS = tuple(range(0, 10)) + (12, 13, 16, 17)
KS_BLOCK, VS_BLOCK, KW_BLOCK, VW_BLOCK = 12, 14, 16, 18

VMEM_LIMIT = 56 * 1024 * 1024


def _params(n_axes):
    return pltpu.CompilerParams(dimension_semantics=("arbitrary",) * n_axes,
                                vmem_limit_bytes=VMEM_LIMIT)


def _dot(a, b, precision=None):
    return jnp.dot(a, b, preferred_element_type=F32, precision=precision)


def _dot_nt(a, b, precision=None):
    return lax.dot_general(a, b, (((1,), (1,)), ((), ())), preferred_element_type=F32,
                           precision=precision)


def _dot_tn(a, b, precision=None):
    return lax.dot_general(a, b, (((0,), (0,)), ((), ())), preferred_element_type=F32,
                           precision=precision)


def _iota(shape, dim):
    return lax.broadcasted_iota(jnp.int32, shape, dim)


def _rmsnorm_kernel(x_ref, g_ref, o_ref):
    x = x_ref[...]
    y = x * lax.rsqrt(jnp.mean(x * x, axis=-1, keepdims=True) + EPS)
    o_ref[...] = (y * g_ref[...]).astype(o_ref.dtype)


def _rmsnorm(x, g, out_dtype, tm=512):
    s, d = x.shape
    return pl.pallas_call(
        _rmsnorm_kernel, grid=(s // tm,),
        in_specs=[pl.BlockSpec((tm, d), lambda i: (i, 0)), pl.BlockSpec((1, d), lambda i: (0, 0))],
        out_specs=pl.BlockSpec((tm, d), lambda i: (i, 0)),
        out_shape=jax.ShapeDtypeStruct((s, d), out_dtype),
        compiler_params=_params(1), name="rmsnorm")(x, g.reshape(1, d))


def _mm_kernel(a_ref, b_ref, o_ref):
    o_ref[...] = _dot(a_ref[...], b_ref[...]).astype(o_ref.dtype)


def _matmul(a, b, out_dtype, tm, tn, name):
    m, k = a.shape
    n = b.shape[1]
    return pl.pallas_call(
        _mm_kernel, grid=(m // tm, n // tn),
        in_specs=[pl.BlockSpec((tm, k), lambda i, j: (i, 0)), pl.BlockSpec((k, tn), lambda i, j: (0, j))],
        out_specs=pl.BlockSpec((tm, tn), lambda i, j: (i, j)),
        out_shape=jax.ShapeDtypeStruct((m, n), out_dtype),
        compiler_params=_params(2), name=name)(a, b)


def _rope_kernel(p_ref, cos_ref, sin_ref, o_ref):
    c = cos_ref[...]
    s = sin_ref[...]
    for b in range(QKV_BLOCKS):
        t = p_ref[:, b * LANES:(b + 1) * LANES]
        if b in ROPE_BLOCKS:
            t = t * c + pltpu.roll(t, HEAD_DIM // 2, 1) * s
        o_ref[:, b * LANES:(b + 1) * LANES] = t.astype(o_ref.dtype)


def _rope(qkv, cosf, sinf, tm=256):
    s = qkv.shape[0]
    return pl.pallas_call(
        _rope_kernel, grid=(s // tm,),
        in_specs=[pl.BlockSpec((tm, QKV_WIDTH), lambda i: (i, 0)),
                  pl.BlockSpec((tm, HEAD_DIM), lambda i: (i, 0)),
                  pl.BlockSpec((tm, HEAD_DIM), lambda i: (i, 0))],
        out_specs=pl.BlockSpec((tm, QKV_WIDTH), lambda i: (i, 0)),
        out_shape=jax.ShapeDtypeStruct((s, QKV_WIDTH), BF16),
        compiler_params=_params(1), name="rope")(qkv, cosf, sinf)


def _compress_kernel(ta_ref, tb_ref, pe_ref, w1_ref, w2_ref, o_ref):
    half = CMP_STRIDE * HEAD_DIM
    w1 = w1_ref[0]
    w1b = w1.astype(BF16)
    pre = _dot(ta_ref[0, 0], w1b[:half]) + _dot(tb_ref[0, 0], w1b[half:])
    pe = jnp.broadcast_to(pe_ref[0], (8, CMP_BLOCK * HEAD_DIM))
    pre = pre + _dot(pe, w1, HI)[0:1]
    h = jax.nn.gelu(pre)
    o_ref[0, 0] = _dot(h.astype(BF16), w2_ref[0].astype(BF16))


def _compress(ta, tb, pe, w1, w2):
    _, g, nc, width = ta.shape
    tok = pl.BlockSpec((1, 1, nc, width), lambda k, gi: (k, gi, 0, 0))
    return pl.pallas_call(
        _compress_kernel, grid=(2, g),
        in_specs=[tok, tok,
                  pl.BlockSpec((1, 1, CMP_BLOCK * HEAD_DIM), lambda k, gi: (k, 0, 0)),
                  pl.BlockSpec((1, CMP_BLOCK * HEAD_DIM, HEAD_DIM), lambda k, gi: (k, 0, 0)),
                  pl.BlockSpec((1, HEAD_DIM, HEAD_DIM), lambda k, gi: (k, 0, 0))],
        out_specs=pl.BlockSpec((1, 1, nc, HEAD_DIM), lambda k, gi: (k, gi, 0, 0)),
        out_shape=jax.ShapeDtypeStruct((2, g, nc, HEAD_DIM), F32),
        compiler_params=_params(2), name="compress")(ta, tb, pe, w1, w2)


def _cmp_kernel(q_ref, kc_ref, vc_ref, ov_ref, o_ref, sel_ref, *, nc, top):
    i = pl.program_id(1)
    kc = kc_ref[0, 0].astype(BF16)
    vc = vc_ref[0, 0].astype(BF16)
    t = i * Q_BLOCK + _iota((Q_BLOCK, 1), 0)
    c_end = _iota((1, nc), 1) * CMP_STRIDE + (CMP_BLOCK - 1)
    cmask = c_end <= t
    psum = jnp.zeros((Q_BLOCK, nc), F32)
    for j in range(HPG):
        sl = slice(j * HEAD_DIM, (j + 1) * HEAD_DIM)
        s = _dot_nt(q_ref[:, sl], kc) * ATTN_SCALE
        s = jnp.where(cmask, s, NEG)
        e = jnp.exp(s - jnp.max(s, axis=-1, keepdims=True))
        p = e / jnp.sum(e, axis=-1, keepdims=True)
        p = jnp.where(cmask, p, 0.0)
        o_ref[:, sl] = _dot(p.astype(BF16), vc).astype(o_ref.dtype)
        psum = psum + p
    imp = _dot(psum, ov_ref[...], HI)

    jb = _iota((1, LANES), 1)
    cur = jnp.right_shift(t, 6)
    forced = (jb == 0) | ((jb <= cur) & (jb > cur - N_LOCAL_SEL))
    visible = jb * SEL_BLOCK <= t
    val = jnp.where(visible, imp + FORCE_BONUS * forced.astype(F32), NEG)
    jbf = jb.astype(F32)
    sel = jnp.zeros((Q_BLOCK, LANES), F32)
    for _ in range(top):
        mx = jnp.max(val, axis=-1, keepdims=True)
        idx = jnp.min(jnp.where(val == mx, jbf, float(LANES)), axis=-1, keepdims=True)
        pick = jbf == idx
        sel = jnp.where(pick, 1.0, sel)
        val = jnp.where(pick, PICKED, val)
    sel_ref[...] = jnp.where(visible, sel, 0.0)


def _cmp_attention(qkv, kvc, overlap, top):
    s = qkv.shape[0]
    nc = kvc.shape[2]
    nqb = s // Q_BLOCK
    gw = HPG * HEAD_DIM
    return pl.pallas_call(
        functools.partial(_cmp_kernel, nc=nc, top=top), grid=(NSA_GROUPS, nqb),
        in_specs=[pl.BlockSpec((Q_BLOCK, gw), lambda g, i: (i, g)),
                  pl.BlockSpec((1, 1, nc, HEAD_DIM), lambda g, i: (0, g, 0, 0)),
                  pl.BlockSpec((1, 1, nc, HEAD_DIM), lambda g, i: (1, g, 0, 0)),
                  pl.BlockSpec((nc, LANES), lambda g, i: (0, 0))],
        out_specs=[pl.BlockSpec((Q_BLOCK, gw), lambda g, i: (i, g)),
                   pl.BlockSpec((Q_BLOCK, LANES), lambda g, i: (i, g))],
        out_shape=[jax.ShapeDtypeStruct((s, NSA_WIDTH), F32),
                   jax.ShapeDtypeStruct((s, NSA_GROUPS * LANES), F32)],
        compiler_params=_params(2), name="cmp_attn")(qkv, kvc, kvc, overlap)


def _slc_kernel(q_ref, k_ref, v_ref, sel_ref, o_ref, *, tk):
    i = pl.program_id(1)
    sel = sel_ref[...].astype(BF16)
    qpos = i * Q_BLOCK + _iota((Q_BLOCK, 1), 0)
    n_tiles = lax.div(i * Q_BLOCK + Q_BLOCK + tk - 1, tk)
    blocks_per_tile = tk // SEL_BLOCK

    def body(kt, carry):
        ms, ls, accs = carry
        start = pl.multiple_of(kt * tk, tk)
        k_t = k_ref[pl.ds(start, tk), :]
        v_t = v_ref[pl.ds(start, tk), :]
        blk = jnp.right_shift(_iota((LANES, tk), 1), 6) + kt * blocks_per_tile
        expand = jnp.where(_iota((LANES, tk), 0) == blk, 1.0, 0.0).astype(BF16)
        chosen = _dot(sel, expand)
        kpos = start + _iota((1, tk), 1)
        allowed = jnp.where(kpos <= qpos, chosen, 0.0) > 0.5
        new_m, new_l, new_acc = [], [], []
        for j in range(HPG):
            s = _dot_nt(q_ref[:, j * HEAD_DIM:(j + 1) * HEAD_DIM], k_t) * ATTN_SCALE
            s = jnp.where(allowed, s, NEG)
            m = jnp.maximum(ms[j], jnp.max(s, axis=-1, keepdims=True))
            alpha = jnp.exp(ms[j] - m)
            p = jnp.exp(s - m)
            new_m.append(m)
            new_l.append(alpha * ls[j] + jnp.sum(p, axis=-1, keepdims=True))
            new_acc.append(alpha * accs[j] + _dot(p.astype(BF16), v_t))
        return tuple(new_m), tuple(new_l), tuple(new_acc)

    init = (tuple(jnp.full((Q_BLOCK, 1), NEG, F32) for _ in range(HPG)),
            tuple(jnp.zeros((Q_BLOCK, 1), F32) for _ in range(HPG)),
            tuple(jnp.zeros((Q_BLOCK, HEAD_DIM), F32) for _ in range(HPG)))
    ms, ls, accs = lax.fori_loop(0, n_tiles, body, init)
    for j in range(HPG):
        out = jnp.where(ms[j] > 0.5 * NEG, accs[j] / ls[j], 0.0)
        o_ref[:, j * HEAD_DIM:(j + 1) * HEAD_DIM] = out.astype(o_ref.dtype)


def _slc_attention(qkv, sel, tk=512):
    s = qkv.shape[0]
    gw = HPG * HEAD_DIM
    return pl.pallas_call(
        functools.partial(_slc_kernel, tk=tk), grid=(NSA_GROUPS, s // Q_BLOCK),
        in_specs=[pl.BlockSpec((Q_BLOCK, gw), lambda g, i: (i, g)),
                  pl.BlockSpec((s, HEAD_DIM), lambda g, i: (0, KS_BLOCK + g)),
                  pl.BlockSpec((s, HEAD_DIM), lambda g, i: (0, VS_BLOCK + g)),
                  pl.BlockSpec((Q_BLOCK, LANES), lambda g, i: (i, g))],
        out_specs=pl.BlockSpec((Q_BLOCK, gw), lambda g, i: (i, g)),
        out_shape=jax.ShapeDtypeStruct((s, NSA_WIDTH), F32),
        compiler_params=_params(2), name="slc_attn")(qkv, qkv, qkv, sel)


def _win_kernel(q_ref, k_ref, v_ref, o_ref):
    i = pl.program_id(1)
    band = WINDOW + Q_BLOCK
    start = pl.multiple_of(jnp.maximum(i * Q_BLOCK - WINDOW, 0), Q_BLOCK)
    k_b = k_ref[pl.ds(start, band), :]
    v_b = v_ref[pl.ds(start, band), :]
    qpos = i * Q_BLOCK + _iota((Q_BLOCK, 1), 0)
    kpos = start + _iota((1, band), 1)
    allowed = (kpos <= qpos) & (kpos > qpos - WINDOW)
    for j in range(HPG):
        sl = slice(j * HEAD_DIM, (j + 1) * HEAD_DIM)
        s = _dot_nt(q_ref[:, sl], k_b) * ATTN_SCALE
        s = jnp.where(allowed, s, NEG)
        e = jnp.exp(s - jnp.max(s, axis=-1, keepdims=True))
        p = e / jnp.sum(e, axis=-1, keepdims=True)
        o_ref[:, sl] = _dot(p.astype(BF16), v_b).astype(o_ref.dtype)


def _win_attention(qkv):
    s = qkv.shape[0]
    gw = HPG * HEAD_DIM
    return pl.pallas_call(
        _win_kernel, grid=(NSA_GROUPS, s // Q_BLOCK),
        in_specs=[pl.BlockSpec((Q_BLOCK, gw), lambda g, i: (i, g)),
                  pl.BlockSpec((s, HEAD_DIM), lambda g, i: (0, KW_BLOCK + g)),
                  pl.BlockSpec((s, HEAD_DIM), lambda g, i: (0, VW_BLOCK + g))],
        out_specs=pl.BlockSpec((Q_BLOCK, gw), lambda g, i: (i, g)),
        out_shape=jax.ShapeDtypeStruct((s, NSA_WIDTH), F32),
        compiler_params=_params(2), name="win_attn")(qkv, qkv, qkv)


def _head_sum(x):
    seg = (jnp.right_shift(_iota((LANES, LANES), 0), 6) ==
           jnp.right_shift(_iota((LANES, LANES), 1), 6)).astype(F32)
    return _dot(x, seg, HI)


def _rwkv_prep_kernel(z_ref, zp_ref, mu_ref, w0_ref, wup_ref, a0_ref, aup_ref, gup_ref, kscale_ref, ka_ref,
                      r_out, lw_out, k_out, v_out, kk_out, a_out, g_out):
    i = pl.program_id(0)
    z = z_ref[...]
    tm = z.shape[0]
    last = jnp.where(i == 0, 0.0, zp_ref[7:8, :])
    z_prev = jnp.where(_iota((tm, 1), 0) == 0, last, pltpu.roll(z, 1, 0))
    z = z + (z_prev - z) * mu_ref[...]
    w = RWKV_WIDTH
    r, k, v = z[:, 0:w], z[:, w:2 * w], z[:, 2 * w:3 * w]
    wd = z[:, 3 * w:3 * w + LORA_PAD]
    ad = z[:, 3 * w + LORA_PAD:3 * w + 2 * LORA_PAD]
    gd = z[:, 3 * w + 2 * LORA_PAD:]
    wlog = -jax.nn.softplus(-(w0_ref[...] + _dot(jnp.tanh(wd), wup_ref[...], HI))) - 0.5
    a = jax.nn.sigmoid(a0_ref[...] + _dot(ad, aup_ref[...], HI))
    g = _dot(jax.nn.sigmoid(gd), gup_ref[...], HI)
    kk = k * kscale_ref[...]
    for b in range(w // LANES):
        sl = slice(b * LANES, (b + 1) * LANES)
        kb = kk[:, sl]
        kk_out[:, sl] = kb * lax.rsqrt(jnp.maximum(_head_sum(kb * kb), 1e-24))
    r_out[...] = r
    lw_out[...] = -jnp.exp(wlog)
    k_out[...] = k * (1.0 + (a - 1.0) * ka_ref[...])
    v_out[...] = v
    a_out[...] = a
    g_out[...] = g


def _rwkv_prep(z, mu, w0, w_up, a0, a_up, g_up, k_k, k_a, tm=256):
    s = z.shape[0]
    w = RWKV_WIDTH
    row = lambda n: pl.BlockSpec((1, n), lambda i: (0, 0))
    full = lambda a: pl.BlockSpec(a.shape, lambda i: (0, 0))
    out = pl.BlockSpec((tm, w), lambda i: (i, 0))
    return pl.pallas_call(
        _rwkv_prep_kernel, grid=(s // tm,),
        in_specs=[pl.BlockSpec((tm, RWKV_Z), lambda i: (i, 0)),
                  pl.BlockSpec((8, RWKV_Z), lambda i: (jnp.maximum(i * (tm // 8) - 1, 0), 0)),
                  row(RWKV_Z), row(w), full(w_up), row(w), full(a_up), full(g_up), row(w), row(w)],
        out_specs=[out] * 7,
        out_shape=[jax.ShapeDtypeStruct((s, w), F32)] * 7,
        compiler_params=_params(1), name="rwkv_prep")(z, z, mu, w0, w_up, a0, a_up, g_up, k_k, k_a)


def _scan_kernel(r_ref, lw_ref, k_ref, v_ref, kk_ref, a_ref, g_ref, rk_ref, lnw_ref, lnb_ref,
                 o_ref, state_ref):
    c = SCAN_CHUNK

    @pl.when(pl.program_id(0) == 0)
    def _():
        state_ref[...] = jnp.zeros_like(state_ref)

    tri = (_iota((c, c), 0) >= _iota((c, c), 1)).astype(F32)
    head0 = _iota((1, LANES), 1) < RWKV_HEAD
    tok_r = jnp.bitwise_and(_iota((LANES, LANES), 0), c - 1)
    tok_c = jnp.bitwise_and(_iota((LANES, LANES), 1), c - 1)
    strict = tok_r > tok_c
    incl = tok_r >= tok_c
    eye = (_iota((LANES, LANES), 0) == _iota((LANES, LANES), 1)).astype(F32)

    def stack(x):
        return jnp.concatenate([jnp.where(head0, x, 0.0), jnp.where(head0, 0.0, x)], axis=0)

    for p in range(RWKV_WIDTH // LANES):
        sl = slice(p * LANES, (p + 1) * LANES)
        lw = lw_ref[:, sl]
        cum = _dot(tri, lw, HI)
        total = cum[c - 1:c, :]
        r, k, v, kk, a = r_ref[:, sl], k_ref[:, sl], v_ref[:, sl], kk_ref[:, sl], a_ref[:, sl]
        b = kk * a
        e_neg = jnp.exp(-cum)
        e_end = jnp.exp(total - cum)
        a2 = stack(-kk * jnp.exp(cum - lw))
        r2 = stack(r * jnp.exp(cum))
        b2 = stack(b * e_neg)
        k2 = stack(k * e_neg)
        v2 = stack(v).astype(BF16)
        a2b, r2b = a2.astype(BF16), r2.astype(BF16)
        gram = _dot_nt(jnp.concatenate([a2b, r2b], axis=0),
                       jnp.concatenate([b2, k2], axis=0).astype(BF16))
        l_ab = jnp.where(strict, gram[0:LANES, 0:LANES], 0.0)
        m_ak = jnp.where(strict, gram[0:LANES, LANES:], 0.0).astype(BF16)
        m_rb = jnp.where(incl, gram[LANES:, 0:LANES], 0.0).astype(BF16)
        m_rk = jnp.where(incl, gram[LANES:, LANES:], 0.0).astype(BF16)
        inv = eye + l_ab
        pw = l_ab
        for _ in range(5):
            pwb = pw.astype(BF16)
            pw = _dot(pwb, pwb)
            inv = inv + _dot(inv.astype(BF16), pw.astype(BF16))
        state = state_ref[p]
        sb = state.astype(BF16)
        x = _dot_nt(a2b, sb) + _dot(m_ak, v2)
        u2 = _dot(inv.astype(BF16), x.astype(BF16)).astype(BF16)
        y2 = _dot_nt(r2b, sb) + _dot(m_rb, u2) + _dot(m_rk, v2)
        y = y2[0:c] + y2[c:]
        uv = jnp.concatenate([u2, v2], axis=0)
        bk = jnp.concatenate([stack(b * e_end), stack(k * e_end)], axis=0).astype(BF16)
        state_ref[p] = state * jnp.exp(total) + _dot_tn(uv, bk)

        mean = _head_sum(y) * (1.0 / RWKV_HEAD)
        d = y - mean
        var = _head_sum(d * d) * (1.0 / RWKV_HEAD)
        out = d * lax.rsqrt(var + GN_EPS) * lnw_ref[:, sl] + lnb_ref[:, sl]
        out = out + _head_sum(r * k * rk_ref[:, sl]) * v
        o_ref[:, sl] = out * g_ref[:, sl]


def _rwkv_scan(r, lw, k, v, kk, a, g, r_k, ln_w, ln_b):
    s, w = r.shape
    blk = pl.BlockSpec((SCAN_CHUNK, w), lambda c: (c, 0))
    row = pl.BlockSpec((1, w), lambda c: (0, 0))
    return pl.pallas_call(
        _scan_kernel, grid=(s // SCAN_CHUNK,),
        in_specs=[blk] * 7 + [row] * 3,
        out_specs=blk,
        out_shape=jax.ShapeDtypeStruct((s, w), F32),
        scratch_shapes=[pltpu.VMEM((w // LANES, LANES, LANES), F32)],
        compiler_params=_params(1), name="rwkv_scan")(r, lw, k, v, kk, a, g, r_k, ln_w, ln_b)


def _merge_kernel(oc_ref, os_ref, ow_ref, bg_ref, ob_ref, mg_ref, pa_ref, pb_ref, o_ref):
    bg = jax.nn.sigmoid(bg_ref[...])
    heads = []
    for h in range(NSA_HEADS):
        sl = slice(h * HEAD_DIM, (h + 1) * HEAD_DIM)
        c0 = NSA_BRANCHES * h
        heads.append(bg[:, c0:c0 + 1] * oc_ref[:, sl] + bg[:, c0 + 1:c0 + 2] * os_ref[:, sl]
                     + bg[:, c0 + 2:c0 + 3] * ow_ref[:, sl])
    o_a = jnp.concatenate(heads, axis=-1).astype(BF16)
    y_a = _dot(o_a, pa_ref[...])
    y_b = _dot(ob_ref[...].astype(BF16), pb_ref[...])
    mg = jax.nn.sigmoid(mg_ref[...])
    o_ref[...] = (mg[:, :D_MODEL] * y_a + mg[:, D_MODEL:] * y_b).astype(o_ref.dtype)


def _merge(o_cmp, o_slc, o_win, bgate, o_b, mgate, proj_a, proj_b, tm=256):
    s = o_cmp.shape[0]
    half = lambda: pl.BlockSpec((tm, NSA_WIDTH), lambda i: (i, 0))
    wspec = pl.BlockSpec((NSA_WIDTH, D_MODEL), lambda i: (0, 0))
    return pl.pallas_call(
        _merge_kernel, grid=(s // tm,),
        in_specs=[half(), half(), half(), pl.BlockSpec((tm, LANES), lambda i: (i, 0)), half(),
                  pl.BlockSpec((tm, 2 * D_MODEL), lambda i: (i, 0)), wspec, wspec],
        out_specs=pl.BlockSpec((tm, D_MODEL), lambda i: (i, 0)),
        out_shape=jax.ShapeDtypeStruct((s, D_MODEL), BF16),
        compiler_params=_params(1), name="merge")(o_cmp, o_slc, o_win, bgate, o_b, mgate, proj_a, proj_b)


def _outproj_kernel(x_ref, m_ref, w_ref, g_ref, x1_ref, h_ref):
    x1 = x_ref[...] + _dot(m_ref[...], w_ref[...])
    x1_ref[...] = x1
    y = x1 * lax.rsqrt(jnp.mean(x1 * x1, axis=-1, keepdims=True) + EPS)
    h_ref[...] = (y * g_ref[...]).astype(h_ref.dtype)


def _outproj(x, mixed, w_out, g, tm=256):
    s, d = x.shape
    blk = pl.BlockSpec((tm, d), lambda i: (i, 0))
    return pl.pallas_call(
        _outproj_kernel, grid=(s // tm,),
        in_specs=[blk, blk, pl.BlockSpec((d, d), lambda i: (0, 0)), pl.BlockSpec((1, d), lambda i: (0, 0))],
        out_specs=[blk, blk],
        out_shape=[jax.ShapeDtypeStruct((s, d), F32), jax.ShapeDtypeStruct((s, d), BF16)],
        compiler_params=_params(1), name="outproj")(x, mixed, w_out, g.reshape(1, d))


def _ffn_up_kernel(h_ref, wg_ref, wv_ref, cwg_ref, cwv_ref, cbg_ref, cbv_ref, o_ref, ug_ref, uv_ref):
    tm = h_ref.shape[0]

    @pl.when(pl.program_id(1) == 0)
    def _():
        ug_ref[0:8, :] = jnp.zeros((8, ug_ref.shape[1]), F32)
        uv_ref[0:8, :] = jnp.zeros((8, uv_ref.shape[1]), F32)

    h = h_ref[...]

    def conv(u_ref, w_ref, cw_ref, cb_ref):
        u_ref[8:8 + tm, :] = _dot(h, w_ref[...])
        out = cb_ref[...]
        for tap in range(CONV_WIDTH):
            off = 8 - (CONV_WIDTH - 1 - tap)
            out = out + u_ref[off:off + tm, :] * cw_ref[tap:tap + 1, :]
        u_ref[0:8, :] = u_ref[tm:tm + 8, :]
        return out

    gate = conv(ug_ref, wg_ref, cwg_ref, cbg_ref)
    val = conv(uv_ref, wv_ref, cwv_ref, cbv_ref)
    o_ref[...] = (jax.nn.silu(gate) * val).astype(o_ref.dtype)


def _ffn_up(h, w_up, conv_w, conv_b, tm=512, tn=512):
    s, d = h.shape
    nj = D_FF // tn
    return pl.pallas_call(
        _ffn_up_kernel, grid=(nj, s // tm),
        in_specs=[pl.BlockSpec((tm, d), lambda j, i: (i, 0)),
                  pl.BlockSpec((d, tn), lambda j, i: (0, j)),
                  pl.BlockSpec((d, tn), lambda j, i: (0, nj + j)),
                  pl.BlockSpec((CONV_WIDTH, tn), lambda j, i: (0, j)),
                  pl.BlockSpec((CONV_WIDTH, tn), lambda j, i: (0, nj + j)),
                  pl.BlockSpec((1, tn), lambda j, i: (0, j)),
                  pl.BlockSpec((1, tn), lambda j, i: (0, nj + j))],
        out_specs=pl.BlockSpec((tm, tn), lambda j, i: (i, j)),
        out_shape=jax.ShapeDtypeStruct((s, D_FF), BF16),
        scratch_shapes=[pltpu.VMEM((tm + 8, tn), F32), pltpu.VMEM((tm + 8, tn), F32)],
        compiler_params=_params(2), name="ffn_up")(h, w_up, w_up, conv_w, conv_w, conv_b, conv_b)


def _ffn_down_kernel(a_ref, w_ref, x_ref, g_ref, o_ref, acc_ref):
    kk = pl.program_id(1)

    @pl.when(kk == 0)
    def _():
        acc_ref[...] = x_ref[...]

    acc_ref[...] += _dot(a_ref[...], w_ref[...])

    @pl.when(kk == pl.num_programs(1) - 1)
    def _():
        x2 = acc_ref[...]
        y = x2 * lax.rsqrt(jnp.mean(x2 * x2, axis=-1, keepdims=True) + EPS)
        o_ref[...] = y * g_ref[...]


def _ffn_down(act, w_down, x1, g, tm=512, tk=512):
    s, d = x1.shape
    return pl.pallas_call(
        _ffn_down_kernel, grid=(s // tm, D_FF // tk),
        in_specs=[pl.BlockSpec((tm, tk), lambda i, k: (i, k)),
                  pl.BlockSpec((tk, d), lambda i, k: (k, 0)),
                  pl.BlockSpec((tm, d), lambda i, k: (i, 0)),
                  pl.BlockSpec((1, d), lambda i, k: (0, 0))],
        out_specs=pl.BlockSpec((tm, d), lambda i, k: (i, 0)),
        out_shape=jax.ShapeDtypeStruct((s, d), F32),
        scratch_shapes=[pltpu.VMEM((tm, d), F32)],
        compiler_params=_params(2), name="ffn_down")(act, w_down, x1, g.reshape(1, d))


def _pad_cols(w, n):
    return jnp.pad(w, ((0, 0), (0, n - w.shape[1])))


def _pad_rows(w, n):
    return jnp.pad(w, ((0, n - w.shape[0]), (0, 0)))


def _split_rwkv_cols(t):
    w = RWKV_WIDTH
    rkv, wd, ad, gd = (t[:, :3 * w], t[:, 3 * w:3 * w + DECAY_LORA],
                       t[:, 3 * w + DECAY_LORA:3 * w + DECAY_LORA + ICLR_LORA],
                       t[:, 3 * w + DECAY_LORA + ICLR_LORA:])
    return jnp.concatenate([rkv, _pad_cols(wd, LORA_PAD), _pad_cols(ad, LORA_PAD), gd], axis=1)


def _rope_tables(seq):
    half = HEAD_DIM // 2
    inv = ROPE_THETA ** (-jnp.arange(half, dtype=F32) / half)
    ang = jnp.arange(seq, dtype=F32)[:, None] * inv[None, :]
    cos, sin = jnp.cos(ang), jnp.sin(ang)
    return jnp.concatenate([cos, cos], axis=1), jnp.concatenate([-sin, sin], axis=1)


def _overlap_matrix(nc):
    c_start = jnp.arange(nc) * CMP_STRIDE
    s_start = jnp.arange(LANES) * SEL_BLOCK
    ov = jnp.clip(jnp.minimum(c_start[:, None] + CMP_BLOCK, s_start[None, :] + SEL_BLOCK)
                  - jnp.maximum(c_start[:, None], s_start[None, :]), 0)
    return ov.astype(F32) / CMP_BLOCK


def _layer(x, norm1_g, w_in, cmp_k_pe, cmp_k_w1, cmp_k_w2, cmp_v_pe, cmp_v_w1, cmp_v_w2, rwkv_mu, rwkv_w0,
           rwkv_w_up, rwkv_a0, rwkv_a_up, rwkv_g_up, rwkv_k_k, rwkv_k_a, rwkv_r_k, rwkv_ln_w, rwkv_ln_b,
           proj_nsa, proj_rwkv, w_out, norm2_g, ffn_up, ffn_conv_w, ffn_conv_b, ffn_down, final_g):
    seq = x.shape[0]
    n_sel = seq // SEL_BLOCK
    assert seq % 2048 == 0 and n_sel <= LANES
    nc = seq // CMP_STRIDE
    w = RWKV_WIDTH

    c_gate = QKV_WIDTH
    c_z = c_gate + NSA_HEADS * NSA_BRANCHES
    c_merge = c_z + 3 * w + DECAY_LORA + ICLR_LORA + GATE_LORA
    w_qkv = w_in[:, :c_gate].astype(BF16)
    w_bgate = _pad_cols(w_in[:, c_gate:c_z], LANES).astype(BF16)
    w_z = _split_rwkv_cols(w_in[:, c_z:c_merge]).astype(BF16)
    w_merge = w_in[:, c_merge:].astype(BF16)

    h = _rmsnorm(x, norm1_g, BF16)
    qkv = _matmul(h, w_qkv, F32, 512, 1280, "proj_qkv")
    bgate = _matmul(h, w_bgate, F32, 512, LANES, "proj_bgate")
    z = _matmul(h, w_z, F32, 512, 896, "proj_rwkv")
    mgate = _matmul(h, w_merge, F32, 512, 1024, "proj_merge")

    cosf, sinf = _rope_tables(seq)
    qkv = _rope(qkv, cosf, sinf)
    kv_c = qkv[:, NSA_WIDTH:NSA_WIDTH + 2 * KV_WIDTH]
    tok = kv_c.reshape(nc, CMP_STRIDE, 2, NSA_GROUPS, HEAD_DIM)
    tok = jnp.transpose(tok, (2, 3, 0, 1, 4)).reshape(2, NSA_GROUPS, nc, CMP_STRIDE * HEAD_DIM)
    tok_next = jnp.concatenate([tok[:, :, 1:], jnp.zeros_like(tok[:, :, :1])], axis=2)
    pe = jnp.stack([cmp_k_pe, cmp_v_pe]).reshape(2, 1, CMP_BLOCK * HEAD_DIM)
    kv_cmp = _compress(tok, tok_next, pe, jnp.stack([cmp_k_w1, cmp_v_w1]), jnp.stack([cmp_k_w2, cmp_v_w2]))
    o_cmp, sel = _cmp_attention(qkv, kv_cmp, _overlap_matrix(nc), min(SEL_TOPK, n_sel))
    o_slc = _slc_attention(qkv, sel)
    o_win = _win_attention(qkv)

    mu = _split_rwkv_cols(rwkv_mu.reshape(1, -1))
    r, lw, k, v, kk, a, g = _rwkv_prep(
        z, mu, rwkv_w0.reshape(1, w), _pad_rows(rwkv_w_up, LORA_PAD), rwkv_a0.reshape(1, w),
        _pad_rows(rwkv_a_up, LORA_PAD), rwkv_g_up, rwkv_k_k.reshape(1, w), rwkv_k_a.reshape(1, w))
    o_b = _rwkv_scan(r, lw, k, v, kk, a, g, rwkv_r_k.reshape(1, w), rwkv_ln_w.reshape(1, w),
                     rwkv_ln_b.reshape(1, w))

    mixed = _merge(o_cmp, o_slc, o_win, bgate, o_b, mgate, proj_nsa.astype(BF16), proj_rwkv.astype(BF16))
    x1, h2 = _outproj(x, mixed, w_out.astype(BF16), norm2_g)
    act = _ffn_up(h2, ffn_up.astype(BF16), ffn_conv_w, ffn_conv_b.reshape(1, -1))
    return _ffn_down(act, ffn_down.astype(BF16), x1, final_g)


def kernel(x, norm1_g, w_in, cmp_k_pe, cmp_k_w1, cmp_k_w2, cmp_v_pe, cmp_v_w1, cmp_v_w2, rwkv_mu, rwkv_w0,
           rwkv_w_up, rwkv_a0, rwkv_a_up, rwkv_g_up, rwkv_k_k, rwkv_k_a, rwkv_r_k, rwkv_ln_w, rwkv_ln_b,
           proj_nsa, proj_rwkv, w_out, norm2_g, ffn_up, ffn_conv_w, ffn_conv_b, ffn_down, final_g):
    assert norm1_g.shape[0] == 1, "single-layer block: the final rmsnorm is fused into the down projection"
    layer = (norm1_g, w_in, cmp_k_pe, cmp_k_w1, cmp_k_w2, cmp_v_pe, cmp_v_w1, cmp_v_w2, rwkv_mu, rwkv_w0,
             rwkv_w_up, rwkv_a0, rwkv_a_up, rwkv_g_up, rwkv_k_k, rwkv_k_a, rwkv_r_k, rwkv_ln_w, rwkv_ln_b,
             proj_nsa, proj_rwkv, w_out, norm2_g, ffn_up, ffn_conv_w, ffn_conv_b, ffn_down)
    weights = [t[0] for t in layer]
    return jnp.stack([_layer(x[b], *weights, final_g) for b in range(x.shape[0])])
```

```python
import functools

import jax
import jax.numpy as jnp
from jax import lax
from jax.experimental import pallas as pl
from jax.experimental.pallas import tpu as pltpu

F32 = jnp.float32
BF16 = jnp.bfloat16
HI = lax.Precision.HIGHEST

LANES = 128
D_MODEL = 2048
HEAD_DIM = 128
NSA_WIDTH = D_MODEL // 2
NSA_HEADS = NSA_WIDTH // HEAD_DIM
NSA_GROUPS = 2
HPG = NSA_HEADS // NSA_GROUPS
KV_WIDTH = NSA_GROUPS * HEAD_DIM
NSA_BRANCHES = 3
CMP_BLOCK = 32
CMP_STRIDE = 16
SEL_BLOCK = 64
SEL_TOPK = 16
N_LOCAL_SEL = 2
WINDOW = 512
Q_BLOCK = 128
ROPE_THETA = 10000.0
ATTN_SCALE = HEAD_DIM ** -0.5
QUERY_SCALE = ATTN_SCALE * 1.4426950408889634
FORCE_BONUS = 1.0e4
NEG = -1.0e30
PICKED = -3.0e38

RWKV_WIDTH = D_MODEL // 2
RWKV_HEAD = 64
DECAY_LORA = 96
ICLR_LORA = 96
GATE_LORA = 256
LORA_PAD = 128
RWKV_Z = 3 * RWKV_WIDTH + 2 * LORA_PAD + GATE_LORA
GN_EPS = 64e-5
SCAN_CHUNK = 64

D_FF = 5632
CONV_WIDTH = 3
EPS = 1e-6

QKV_WIDTH = NSA_WIDTH + 6 * KV_WIDTH
QKV_BLOCKS = QKV_WIDTH // LANES
ROPE_BLOCKS = tuple(range(0, 10)) + (12, 13, 16, 17)
KS_BLOCK, VS_BLOCK, KW_BLOCK, VW_BLOCK = 12, 14, 16, 18

VMEM_LIMIT = 56 * 1024 * 1024


def _params(n_axes):
    return pltpu.CompilerParams(dimension_semantics=("arbitrary",) * n_axes,
                                vmem_limit_bytes=VMEM_LIMIT)


def _dot(a, b, precision=None):
    return jnp.dot(a, b, preferred_element_type=F32, precision=precision)


def _dot_nt(a, b, precision=None):
    return lax.dot_general(a, b, (((1,), (1,)), ((), ())), preferred_element_type=F32,
                           precision=precision)


def _dot_tn(a, b, precision=None):
    return lax.dot_general(a, b, (((0,), (0,)), ((), ())), preferred_element_type=F32,
                           precision=precision)


def _iota(shape, dim):
    return lax.broadcasted_iota(jnp.int32, shape, dim)


def _rmsnorm_kernel(x_ref, g_ref, o_ref):
    x = x_ref[...]
    y = x * lax.rsqrt(jnp.mean(x * x, axis=-1, keepdims=True) + EPS)
    o_ref[...] = (y * g_ref[...]).astype(o_ref.dtype)


def _rmsnorm(x, g, out_dtype, tm=512):
    s, d = x.shape
    return pl.pallas_call(
        _rmsnorm_kernel, grid=(s // tm,),
        in_specs=[pl.BlockSpec((tm, d), lambda i: (i, 0)), pl.BlockSpec((1, d), lambda i: (0, 0))],
        out_specs=pl.BlockSpec((tm, d), lambda i: (i, 0)),
        out_shape=jax.ShapeDtypeStruct((s, d), out_dtype),
        compiler_params=_params(1), name="rmsnorm")(x, g.reshape(1, d))


def _mm_kernel(a_ref, b_ref, o_ref):
    o_ref[...] = _dot(a_ref[...], b_ref[...]).astype(o_ref.dtype)


def _matmul(a, b, out_dtype, tm, tn, name):
    m, k = a.shape
    n = b.shape[1]
    return pl.pallas_call(
        _mm_kernel, grid=(m // tm, n // tn),
        in_specs=[pl.BlockSpec((tm, k), lambda i, j: (i, 0)), pl.BlockSpec((k, tn), lambda i, j: (0, j))],
        out_specs=pl.BlockSpec((tm, tn), lambda i, j: (i, j)),
        out_shape=jax.ShapeDtypeStruct((m, n), out_dtype),
        compiler_params=_params(2), name=name)(a, b)


def _rope_kernel(p_ref, cos_ref, sin_ref, o_ref):
    c = cos_ref[...]
    s = sin_ref[...]
    for b in range(QKV_BLOCKS):
        t = p_ref[:, b * LANES:(b + 1) * LANES]
        if b in ROPE_BLOCKS:
            t = t * c + pltpu.roll(t, HEAD_DIM // 2, 1) * s
        if b < NSA_HEADS:
            t = t * QUERY_SCALE
        o_ref[:, b * LANES:(b + 1) * LANES] = t.astype(o_ref.dtype)


def _rope(qkv, cosf, sinf, tm=256):
    s = qkv.shape[0]
    return pl.pallas_call(
        _rope_kernel, grid=(s // tm,),
        in_specs=[pl.BlockSpec((tm, QKV_WIDTH), lambda i: (i, 0)),
                  pl.BlockSpec((tm, HEAD_DIM), lambda i: (i, 0)),
                  pl.BlockSpec((tm, HEAD_DIM), lambda i: (i, 0))],
        out_specs=pl.BlockSpec((tm, QKV_WIDTH), lambda i: (i, 0)),
        out_shape=jax.ShapeDtypeStruct((s, QKV_WIDTH), BF16),
        compiler_params=_params(1), name="rope")(qkv, cosf, sinf)


def _compress_kernel(ta_ref, tb_ref, pe_ref, w1_ref, w2_ref, o_ref):
    half = CMP_STRIDE * HEAD_DIM
    w1 = w1_ref[0]
    w1b = w1.astype(BF16)
    pre = _dot(ta_ref[0, 0], w1b[:half]) + _dot(tb_ref[0, 0], w1b[half:])
    pe = jnp.broadcast_to(pe_ref[0], (8, CMP_BLOCK * HEAD_DIM))
    pre = pre + _dot(pe, w1, HI)[0:1]
    h = jax.nn.gelu(pre)
    o_ref[0, 0] = _dot(h.astype(BF16), w2_ref[0].astype(BF16))


def _compress(ta, tb, pe, w1, w2):
    _, g, nc, width = ta.shape
    tok = pl.BlockSpec((1, 1, nc, width), lambda k, gi: (k, gi, 0, 0))
    return pl.pallas_call(
        _compress_kernel, grid=(2, g),
        in_specs=[tok, tok,
                  pl.BlockSpec((1, 1, CMP_BLOCK * HEAD_DIM), lambda k, gi: (k, 0, 0)),
                  pl.BlockSpec((1, CMP_BLOCK * HEAD_DIM, HEAD_DIM), lambda k, gi: (k, 0, 0)),
                  pl.BlockSpec((1, HEAD_DIM, HEAD_DIM), lambda k, gi: (k, 0, 0))],
        out_specs=pl.BlockSpec((1, 1, nc, HEAD_DIM), lambda k, gi: (k, gi, 0, 0)),
        out_shape=jax.ShapeDtypeStruct((2, g, nc, HEAD_DIM), F32),
        compiler_params=_params(2), name="compress")(ta, tb, pe, w1, w2)


def _stack_heads(q_ref):
    return jnp.concatenate([q_ref[:, j * HEAD_DIM:(j + 1) * HEAD_DIM] for j in range(HPG)], axis=0)


def _store_heads(o_ref, o):
    for j in range(HPG):
        o_ref[:, j * HEAD_DIM:(j + 1) * HEAD_DIM] = o[j * Q_BLOCK:(j + 1) * Q_BLOCK].astype(o_ref.dtype)


def _cmp_kernel(q_ref, kc_ref, vc_ref, ov_ref, o_ref, sel_ref, *, nc, top):
    i = pl.program_id(1)
    kc = kc_ref[0, 0].astype(BF16)
    vc = vc_ref[0, 0].astype(BF16)
    t = i * Q_BLOCK + _iota((Q_BLOCK, 1), 0)
    c_end = _iota((1, nc), 1) * CMP_STRIDE + (CMP_BLOCK - 1)
    cmask = (c_end <= t)[None]
    s = _dot_nt(_stack_heads(q_ref), kc).reshape(HPG, Q_BLOCK, nc)
    s = jnp.where(cmask, s, NEG)
    e = jnp.exp2(s - jnp.max(s, axis=-1, keepdims=True))
    p = e / jnp.sum(e, axis=-1, keepdims=True)
    p = jnp.where(cmask, p, 0.0)
    _store_heads(o_ref, _dot(p.reshape(HPG * Q_BLOCK, nc).astype(BF16), vc))
    imp = _dot(jnp.sum(p, axis=0), ov_ref[...], HI)

    jb = _iota((1, LANES), 1)
    cur = jnp.right_shift(t, 6)
    forced = (jb == 0) | ((jb <= cur) & (jb > cur - N_LOCAL_SEL))
    visible = jb * SEL_BLOCK <= t
    val = jnp.where(visible, imp + FORCE_BONUS * forced.astype(F32), NEG)
    jbf = jb.astype(F32)
    sel = jnp.zeros((Q_BLOCK, LANES), F32)
    for _ in range(top):
        mx = jnp.max(val, axis=-1, keepdims=True)
        idx = jnp.min(jnp.where(val == mx, jbf, float(LANES)), axis=-1, keepdims=True)
        pick = jbf == idx
        sel = jnp.where(pick, 1.0, sel)
        val = jnp.where(pick, PICKED, val)
    sel_ref[...] = jnp.where(visible, sel, 0.0)


def _cmp_attention(qkv, kvc, overlap, top):
    s = qkv.shape[0]
    nc = kvc.shape[2]
    nqb = s // Q_BLOCK
    gw = HPG * HEAD_DIM
    return pl.pallas_call(
        functools.partial(_cmp_kernel, nc=nc, top=top), grid=(NSA_GROUPS, nqb),
        in_specs=[pl.BlockSpec((Q_BLOCK, gw), lambda g, i: (i, g)),
                  pl.BlockSpec((1, 1, nc, HEAD_DIM), lambda g, i: (0, g, 0, 0)),
                  pl.BlockSpec((1, 1, nc, HEAD_DIM), lambda g, i: (1, g, 0, 0)),
                  pl.BlockSpec((nc, LANES), lambda g, i: (0, 0))],
        out_specs=[pl.BlockSpec((Q_BLOCK, gw), lambda g, i: (i, g)),
                   pl.BlockSpec((Q_BLOCK, LANES), lambda g, i: (i, g))],
        out_shape=[jax.ShapeDtypeStruct((s, NSA_WIDTH), F32),
                   jax.ShapeDtypeStruct((s, NSA_GROUPS * LANES), F32)],
        compiler_params=_params(2), name="cmp_attn")(qkv, kvc, kvc, overlap)


def _slc_kernel(q_ref, k_ref, v_ref, sel_ref, o_ref, m_ref, l_ref, acc_ref, *, tk):
    i = pl.program_id(1)
    rows = HPG * Q_BLOCK
    q = _stack_heads(q_ref)
    sel = sel_ref[...].astype(BF16)
    qpos = i * Q_BLOCK + _iota((Q_BLOCK, 1), 0)
    blk_off = _iota((LANES, tk), 0) - jnp.right_shift(_iota((LANES, tk), 1), 6)
    blocks_per_tile = tk // SEL_BLOCK
    m_ref[...] = jnp.full(m_ref.shape, NEG, F32)
    l_ref[...] = jnp.zeros(l_ref.shape, F32)
    acc_ref[...] = jnp.zeros(acc_ref.shape, F32)
    last = lax.div(i * Q_BLOCK + Q_BLOCK - 1, tk)

    def tile(kt, causal):
        start = pl.multiple_of(kt * tk, tk)
        k_t = k_ref[pl.ds(start, tk), :]
        v_t = v_ref[pl.ds(start, tk), :]
        expand = jnp.where(blk_off == kt * blocks_per_tile, 1.0, 0.0).astype(BF16)
        chosen = _dot(sel, expand)
        if causal:
            chosen = jnp.where(start + _iota((1, tk), 1) <= qpos, chosen, 0.0)
        allowed = (chosen > 0.5)[None]
        s = _dot_nt(q, k_t).reshape(HPG, Q_BLOCK, tk)
        s = jnp.where(allowed, s, NEG)
        m_old = m_ref[...]
        m_new = jnp.maximum(m_old, jnp.max(s, axis=-1, keepdims=True))
        alpha = jnp.exp2(m_old - m_new)
        p = jnp.exp2(s - m_new)
        m_ref[...] = m_new
        l_ref[...] = alpha * l_ref[...] + jnp.sum(p, axis=-1, keepdims=True)
        pv = _dot(p.reshape(rows, tk).astype(BF16), v_t)
        acc_ref[...] = alpha.reshape(rows, 1) * acc_ref[...] + pv

    def body(kt, carry):
        tile(kt, False)
        return carry

    lax.fori_loop(0, last, body, 0)
    tile(last, True)
    m = m_ref[...].reshape(rows, 1)
    out = jnp.where(m > 0.5 * NEG, acc_ref[...] / l_ref[...].reshape(rows, 1), 0.0)
    _store_heads(o_ref, out)


def _slc_attention(qkv, sel, tk=512):
    s = qkv.shape[0]
    gw = HPG * HEAD_DIM
    return pl.pallas_call(
        functools.partial(_slc_kernel, tk=tk), grid=(NSA_GROUPS, s // Q_BLOCK),
        in_specs=[pl.BlockSpec((Q_BLOCK, gw), lambda g, i: (i, g)),
                  pl.BlockSpec((s, HEAD_DIM), lambda g, i: (0, KS_BLOCK + g)),
                  pl.BlockSpec((s, HEAD_DIM), lambda g, i: (0, VS_BLOCK + g)),
                  pl.BlockSpec((Q_BLOCK, LANES), lambda g, i: (i, g))],
        out_specs=pl.BlockSpec((Q_BLOCK, gw), lambda g, i: (i, g)),
        out_shape=jax.ShapeDtypeStruct((s, NSA_WIDTH), F32),
        scratch_shapes=[pltpu.VMEM((HPG, Q_BLOCK, 1), F32), pltpu.VMEM((HPG, Q_BLOCK, 1), F32),
                        pltpu.VMEM((HPG * Q_BLOCK, HEAD_DIM), F32)],
        compiler_params=_params(2), name="slc_attn")(qkv, qkv, qkv, sel)


def _win_kernel(q_ref, k_ref, v_ref, o_ref):
    i = pl.program_id(1)
    band = WINDOW + Q_BLOCK
    start = pl.multiple_of(jnp.maximum(i * Q_BLOCK - WINDOW, 0), Q_BLOCK)
    k_b = k_ref[pl.ds(start, band), :]
    v_b = v_ref[pl.ds(start, band), :]
    qpos = i * Q_BLOCK + _iota((Q_BLOCK, 1), 0)
    kpos = start + _iota((1, band), 1)
    allowed = ((kpos <= qpos) & (kpos > qpos - WINDOW))[None]
    s = _dot_nt(_stack_heads(q_ref), k_b).reshape(HPG, Q_BLOCK, band)
    s = jnp.where(allowed, s, NEG)
    e = jnp.exp2(s - jnp.max(s, axis=-1, keepdims=True))
    p = e / jnp.sum(e, axis=-1, keepdims=True)
    _store_heads(o_ref, _dot(p.reshape(HPG * Q_BLOCK, band).astype(BF16), v_b))


def _win_attention(qkv):
    s = qkv.shape[0]
    gw = HPG * HEAD_DIM
    return pl.pallas_call(
        _win_kernel, grid=(NSA_GROUPS, s // Q_BLOCK),
        in_specs=[pl.BlockSpec((Q_BLOCK, gw), lambda g, i: (i, g)),
                  pl.BlockSpec((s, HEAD_DIM), lambda g, i: (0, KW_BLOCK + g)),
                  pl.BlockSpec((s, HEAD_DIM), lambda g, i: (0, VW_BLOCK + g))],
        out_specs=pl.BlockSpec((Q_BLOCK, gw), lambda g, i: (i, g)),
        out_shape=jax.ShapeDtypeStruct((s, NSA_WIDTH), F32),
        compiler_params=_params(2), name="win_attn")(qkv, qkv, qkv)


def _head_sum(x):
    seg = (jnp.right_shift(_iota((LANES, LANES), 0), 6) ==
           jnp.right_shift(_iota((LANES, LANES), 1), 6)).astype(F32)
    return _dot(x, seg, HI)


def _rwkv_prep_kernel(z_ref, zp_ref, mu_ref, w0_ref, wup_ref, a0_ref, aup_ref, gup_ref, kscale_ref, ka_ref,
                      r_out, lw_out, k_out, v_out, kk_out, a_out, g_out):
    i = pl.program_id(0)
    z = z_ref[...]
    tm = z.shape[0]
    last = jnp.where(i == 0, 0.0, zp_ref[7:8, :])
    z_prev = jnp.where(_iota((tm, 1), 0) == 0, last, pltpu.roll(z, 1, 0))
    z = z + (z_prev - z) * mu_ref[...]
    w = RWKV_WIDTH
    r, k, v = z[:, 0:w], z[:, w:2 * w], z[:, 2 * w:3 * w]
    wd = z[:, 3 * w:3 * w + LORA_PAD]
    ad = z[:, 3 * w + LORA_PAD:3 * w + 2 * LORA_PAD]
    gd = z[:, 3 * w + 2 * LORA_PAD:]
    wlog = -jax.nn.softplus(-(w0_ref[...] + _dot(jnp.tanh(wd), wup_ref[...], HI))) - 0.5
    a = jax.nn.sigmoid(a0_ref[...] + _dot(ad, aup_ref[...], HI))
    g = _dot(jax.nn.sigmoid(gd), gup_ref[...], HI)
    kk = k * kscale_ref[...]
    for b in range(w // LANES):
        sl = slice(b * LANES, (b + 1) * LANES)
        kb = kk[:, sl]
        kk_out[:, sl] = kb * lax.rsqrt(jnp.maximum(_head_sum(kb * kb), 1e-24))
    r_out[...] = r
    lw_out[...] = -jnp.exp(wlog)
    k_out[...] = k * (1.0 + (a - 1.0) * ka_ref[...])
    v_out[...] = v
    a_out[...] = a
    g_out[...] = g


def _rwkv_prep(z, mu, w0, w_up, a0, a_up, g_up, k_k, k_a, tm=256):
    s = z.shape[0]
    w = RWKV_WIDTH
    row = lambda n: pl.BlockSpec((1, n), lambda i: (0, 0))
    full = lambda a: pl.BlockSpec(a.shape, lambda i: (0, 0))
    out = pl.BlockSpec((tm, w), lambda i: (i, 0))
    return pl.pallas_call(
        _rwkv_prep_kernel, grid=(s // tm,),
        in_specs=[pl.BlockSpec((tm, RWKV_Z), lambda i: (i, 0)),
                  pl.BlockSpec((8, RWKV_Z), lambda i: (jnp.maximum(i * (tm // 8) - 1, 0), 0)),
                  row(RWKV_Z), row(w), full(w_up), row(w), full(a_up), full(g_up), row(w), row(w)],
        out_specs=[out] * 7,
        out_shape=[jax.ShapeDtypeStruct((s, w), F32)] * 7,
        compiler_params=_params(1), name="rwkv_prep")(z, z, mu, w0, w_up, a0, a_up, g_up, k_k, k_a)


def _scan_kernel(r_ref, lw_ref, k_ref, v_ref, kk_ref, a_ref, g_ref, rk_ref, lnw_ref, lnb_ref,
                 o_ref, state_ref):
    c = SCAN_CHUNK

    @pl.when(pl.program_id(0) == 0)
    def _():
        state_ref[...] = jnp.zeros_like(state_ref)

    tri = (_iota((c, c), 0) >= _iota((c, c), 1)).astype(F32)
    head0 = _iota((1, LANES), 1) < RWKV_HEAD
    tok_r = jnp.bitwise_and(_iota((LANES, LANES), 0), c - 1)
    tok_c = jnp.bitwise_and(_iota((LANES, LANES), 1), c - 1)
    strict = tok_r > tok_c
    incl = tok_r >= tok_c
    eye = (_iota((LANES, LANES), 0) == _iota((LANES, LANES), 1)).astype(F32)

    def stack(x):
        return jnp.concatenate([jnp.where(head0, x, 0.0), jnp.where(head0, 0.0, x)], axis=0)

    pairs = range(RWKV_WIDTH // LANES)
    sls = [slice(p * LANES, (p + 1) * LANES) for p in pairs]
    lw = [lw_ref[:, sl] for sl in sls]
    cum = [_dot(tri, x, HI) for x in lw]
    total = [x[c - 1:c, :] for x in cum]
    r = [r_ref[:, sl] for sl in sls]
    k = [k_ref[:, sl] for sl in sls]
    v = [v_ref[:, sl] for sl in sls]
    kk = [kk_ref[:, sl] for sl in sls]
    b = [kk[p] * a_ref[:, sls[p]] for p in pairs]
    e_neg = [jnp.exp(-x) for x in cum]
    a2b = [stack(-kk[p] * jnp.exp(cum[p] - lw[p])).astype(BF16) for p in pairs]
    r2b = [stack(r[p] * jnp.exp(cum[p])).astype(BF16) for p in pairs]
    b2 = [stack(b[p] * e_neg[p]) for p in pairs]
    k2 = [stack(k[p] * e_neg[p]) for p in pairs]
    v2 = [stack(x).astype(BF16) for x in v]
    gram = [_dot_nt(jnp.concatenate([a2b[p], r2b[p]], axis=0),
                    jnp.concatenate([b2[p], k2[p]], axis=0).astype(BF16)) for p in pairs]
    l_ab = [jnp.where(strict, x[0:LANES, 0:LANES], 0.0) for x in gram]
    m_ak = [jnp.where(strict, x[0:LANES, LANES:], 0.0).astype(BF16) for x in gram]
    m_rb = [jnp.where(incl, x[LANES:, 0:LANES], 0.0).astype(BF16) for x in gram]
    m_rk = [jnp.where(incl, x[LANES:, LANES:], 0.0).astype(BF16) for x in gram]
    inv = [eye + x for x in l_ab]
    pw = l_ab
    for _ in range(5):
        pwb = [x.astype(BF16) for x in pw]
        pw = [_dot(x, x) for x in pwb]
        inv = [inv[p] + _dot(inv[p].astype(BF16), pw[p].astype(BF16)) for p in pairs]
    state = [state_ref[p] for p in pairs]
    sb = [x.astype(BF16) for x in state]
    x = [_dot_nt(a2b[p], sb[p]) + _dot(m_ak[p], v2[p]) for p in pairs]
    u2 = [_dot(inv[p].astype(BF16), x[p].astype(BF16)).astype(BF16) for p in pairs]
    y2 = [_dot_nt(r2b[p], sb[p]) + _dot(m_rb[p], u2[p]) + _dot(m_rk[p], v2[p]) for p in pairs]
    for p in pairs:
        e_end = jnp.exp(total[p] - cum[p])
        uv = jnp.concatenate([u2[p], v2[p]], axis=0)
        bk = jnp.concatenate([stack(b[p] * e_end), stack(k[p] * e_end)], axis=0).astype(BF16)
        state_ref[p] = state[p] * jnp.exp(total[p]) + _dot_tn(uv, bk)
    y = [x[0:c] + x[c:] for x in y2]
    mean = [_head_sum(x) * (1.0 / RWKV_HEAD) for x in y]
    d = [y[p] - mean[p] for p in pairs]
    var = [_head_sum(x * x) * (1.0 / RWKV_HEAD) for x in d]
    bonus = [_head_sum(r[p] * k[p] * rk_ref[:, sls[p]]) for p in pairs]
    for p in pairs:
        sl = sls[p]
        out = d[p] * lax.rsqrt(var[p] + GN_EPS) * lnw_ref[:, sl] + lnb_ref[:, sl]
        o_ref[:, sl] = (out + bonus[p] * v[p]) * g_ref[:, sl]


def _rwkv_scan(r, lw, k, v, kk, a, g, r_k, ln_w, ln_b):
    s, w = r.shape
    blk = pl.BlockSpec((SCAN_CHUNK, w), lambda c: (c, 0))
    row = pl.BlockSpec((1, w), lambda c: (0, 0))
    return pl.pallas_call(
        _scan_kernel, grid=(s // SCAN_CHUNK,),
        in_specs=[blk] * 7 + [row] * 3,
        out_specs=blk,
        out_shape=jax.ShapeDtypeStruct((s, w), F32),
        scratch_shapes=[pltpu.VMEM((w // LANES, LANES, LANES), F32)],
        compiler_params=_params(1), name="rwkv_scan")(r, lw, k, v, kk, a, g, r_k, ln_w, ln_b)


def _merge_kernel(oc_ref, os_ref, ow_ref, bg_ref, ob_ref, mg_ref, pa_ref, pb_ref, o_ref):
    bg = jax.nn.sigmoid(bg_ref[...])
    heads = []
    for h in range(NSA_HEADS):
        sl = slice(h * HEAD_DIM, (h + 1) * HEAD_DIM)
        c0 = NSA_BRANCHES * h
        heads.append(bg[:, c0:c0 + 1] * oc_ref[:, sl] + bg[:, c0 + 1:c0 + 2] * os_ref[:, sl]
                     + bg[:, c0 + 2:c0 + 3] * ow_ref[:, sl])
    o_a = jnp.concatenate(heads, axis=-1).astype(BF16)
    y_a = _dot(o_a, pa_ref[...])
    y_b = _dot(ob_ref[...].astype(BF16), pb_ref[...])
    mg = jax.nn.sigmoid(mg_ref[...])
    o_ref[...] = (mg[:, :D_MODEL] * y_a + mg[:, D_MODEL:] * y_b).astype(o_ref.dtype)


def _merge(o_cmp, o_slc, o_win, bgate, o_b, mgate, proj_a, proj_b, tm=256):
    s = o_cmp.shape[0]
    half = lambda: pl.BlockSpec((tm, NSA_WIDTH), lambda i: (i, 0))
    wspec = pl.BlockSpec((NSA_WIDTH, D_MODEL), lambda i: (0, 0))
    return pl.pallas_call(
        _merge_kernel, grid=(s // tm,),
        in_specs=[half(), half(), half(), pl.BlockSpec((tm, LANES), lambda i: (i, 0)), half(),
                  pl.BlockSpec((tm, 2 * D_MODEL), lambda i: (i, 0)), wspec, wspec],
        out_specs=pl.BlockSpec((tm, D_MODEL), lambda i: (i, 0)),
        out_shape=jax.ShapeDtypeStruct((s, D_MODEL), BF16),
        compiler_params=_params(1), name="merge")(o_cmp, o_slc, o_win, bgate, o_b, mgate, proj_a, proj_b)


def _outproj_kernel(x_ref, m_ref, w_ref, g_ref, x1_ref, h_ref):
    x1 = x_ref[...] + _dot(m_ref[...], w_ref[...])
    x1_ref[...] = x1
    y = x1 * lax.rsqrt(jnp.mean(x1 * x1, axis=-1, keepdims=True) + EPS)
    h_ref[...] = (y * g_ref[...]).astype(h_ref.dtype)


def _outproj(x, mixed, w_out, g, tm=256):
    s, d = x.shape
    blk = pl.BlockSpec((tm, d), lambda i: (i, 0))
    return pl.pallas_call(
        _outproj_kernel, grid=(s // tm,),
        in_specs=[blk, blk, pl.BlockSpec((d, d), lambda i: (0, 0)), pl.BlockSpec((1, d), lambda i: (0, 0))],
        out_specs=[blk, blk],
        out_shape=[jax.ShapeDtypeStruct((s, d), F32), jax.ShapeDtypeStruct((s, d), BF16)],
        compiler_params=_params(1), name="outproj")(x, mixed, w_out, g.reshape(1, d))


def _ffn_up_kernel(h_ref, wg_ref, wv_ref, cwg_ref, cwv_ref, cbg_ref, cbv_ref, o_ref, ug_ref, uv_ref):
    tm = h_ref.shape[0]

    @pl.when(pl.program_id(1) == 0)
    def _():
        ug_ref[0:8, :] = jnp.zeros((8, ug_ref.shape[1]), F32)
        uv_ref[0:8, :] = jnp.zeros((8, uv_ref.shape[1]), F32)

    h = h_ref[...]

    def conv(u_ref, w_ref, cw_ref, cb_ref):
        u_ref[8:8 + tm, :] = _dot(h, w_ref[...])
        out = cb_ref[...]
        for tap in range(CONV_WIDTH):
            off = 8 - (CONV_WIDTH - 1 - tap)
            out = out + u_ref[off:off + tm, :] * cw_ref[tap:tap + 1, :]
        u_ref[0:8, :] = u_ref[tm:tm + 8, :]
        return out

    gate = conv(ug_ref, wg_ref, cwg_ref, cbg_ref)
    val = conv(uv_ref, wv_ref, cwv_ref, cbv_ref)
    o_ref[...] = (jax.nn.silu(gate) * val).astype(o_ref.dtype)


def _ffn_up(h, w_up, conv_w, conv_b, tm=512, tn=512):
    s, d = h.shape
    nj = D_FF // tn
    return pl.pallas_call(
        _ffn_up_kernel, grid=(nj, s // tm),
        in_specs=[pl.BlockSpec((tm, d), lambda j, i: (i, 0)),
                  pl.BlockSpec((d, tn), lambda j, i: (0, j)),
                  pl.BlockSpec((d, tn), lambda j, i: (0, nj + j)),
                  pl.BlockSpec((CONV_WIDTH, tn), lambda j, i: (0, j)),
                  pl.BlockSpec((CONV_WIDTH, tn), lambda j, i: (0, nj + j)),
                  pl.BlockSpec((1, tn), lambda j, i: (0, j)),
                  pl.BlockSpec((1, tn), lambda j, i: (0, nj + j))],
        out_specs=pl.BlockSpec((tm, tn), lambda j, i: (i, j)),
        out_shape=jax.ShapeDtypeStruct((s, D_FF), BF16),
        scratch_shapes=[pltpu.VMEM((tm + 8, tn), F32), pltpu.VMEM((tm + 8, tn), F32)],
        compiler_params=_params(2), name="ffn_up")(h, w_up, w_up, conv_w, conv_w, conv_b, conv_b)


def _ffn_down_kernel(a_ref, w_ref, x_ref, g_ref, o_ref, acc_ref):
    kk = pl.program_id(1)

    @pl.when(kk == 0)
    def _():
        acc_ref[...] = x_ref[...]

    acc_ref[...] += _dot(a_ref[...], w_ref[...])

    @pl.when(kk == pl.num_programs(1) - 1)
    def _():
        x2 = acc_ref[...]
        y = x2 * lax.rsqrt(jnp.mean(x2 * x2, axis=-1, keepdims=True) + EPS)
        o_ref[...] = y * g_ref[...]


def _ffn_down(act, w_down, x1, g, tm=512, tk=512):
    s, d = x1.shape
    return pl.pallas_call(
        _ffn_down_kernel, grid=(s // tm, D_FF // tk),
        in_specs=[pl.BlockSpec((tm, tk), lambda i, k: (i, k)),
                  pl.BlockSpec((tk, d), lambda i, k: (k, 0)),
                  pl.BlockSpec((tm, d), lambda i, k: (i, 0)),
                  pl.BlockSpec((1, d), lambda i, k: (0, 0))],
        out_specs=pl.BlockSpec((tm, d), lambda i, k: (i, 0)),
        out_shape=jax.ShapeDtypeStruct((s, d), F32),
        scratch_shapes=[pltpu.VMEM((tm, d), F32)],
        compiler_params=_params(2), name="ffn_down")(act, w_down, x1, g.reshape(1, d))


def _pad_cols(w, n):
    return jnp.pad(w, ((0, 0), (0, n - w.shape[1])))


def _pad_rows(w, n):
    return jnp.pad(w, ((0, n - w.shape[0]), (0, 0)))


def _split_rwkv_cols(t):
    w = RWKV_WIDTH
    rkv, wd, ad, gd = (t[:, :3 * w], t[:, 3 * w:3 * w + DECAY_LORA],
                       t[:, 3 * w + DECAY_LORA:3 * w + DECAY_LORA + ICLR_LORA],
                       t[:, 3 * w + DECAY_LORA + ICLR_LORA:])
    return jnp.concatenate([rkv, _pad_cols(wd, LORA_PAD), _pad_cols(ad, LORA_PAD), gd], axis=1)


def _rope_tables(seq):
    half = HEAD_DIM // 2
    inv = ROPE_THETA ** (-jnp.arange(half, dtype=F32) / half)
    ang = jnp.arange(seq, dtype=F32)[:, None] * inv[None, :]
    cos, sin = jnp.cos(ang), jnp.sin(ang)
    return jnp.concatenate([cos, cos], axis=1), jnp.concatenate([-sin, sin], axis=1)


def _overlap_matrix(nc):
    c_start = jnp.arange(nc) * CMP_STRIDE
    s_start = jnp.arange(LANES) * SEL_BLOCK
    ov = jnp.clip(jnp.minimum(c_start[:, None] + CMP_BLOCK, s_start[None, :] + SEL_BLOCK)
                  - jnp.maximum(c_start[:, None], s_start[None, :]), 0)
    return ov.astype(F32) / CMP_BLOCK


def _layer(x, norm1_g, w_in, cmp_k_pe, cmp_k_w1, cmp_k_w2, cmp_v_pe, cmp_v_w1, cmp_v_w2, rwkv_mu, rwkv_w0,
           rwkv_w_up, rwkv_a0, rwkv_a_up, rwkv_g_up, rwkv_k_k, rwkv_k_a, rwkv_r_k, rwkv_ln_w, rwkv_ln_b,
           proj_nsa, proj_rwkv, w_out, norm2_g, ffn_up, ffn_conv_w, ffn_conv_b, ffn_down, final_g):
    seq = x.shape[0]
    n_sel = seq // SEL_BLOCK
    assert seq % 2048 == 0 and n_sel <= LANES
    nc = seq // CMP_STRIDE
    w = RWKV_WIDTH

    c_gate = QKV_WIDTH
    c_z = c_gate + NSA_HEADS * NSA_BRANCHES
    c_merge = c_z + 3 * w + DECAY_LORA + ICLR_LORA + GATE_LORA
    w_qkv = w_in[:, :c_gate].astype(BF16)
    w_bgate = _pad_cols(w_in[:, c_gate:c_z], LANES).astype(BF16)
    w_z = _split_rwkv_cols(w_in[:, c_z:c_merge]).astype(BF16)
    w_merge = w_in[:, c_merge:].astype(BF16)

    h = _rmsnorm(x, norm1_g, BF16)
    qkv = _matmul(h, w_qkv, F32, 512, 1280, "proj_qkv")
    bgate = _matmul(h, w_bgate, F32, 512, LANES, "proj_bgate")
    z = _matmul(h, w_z, F32, 512, 896, "proj_rwkv")
    mgate = _matmul(h, w_merge, F32, 512, 1024, "proj_merge")

    cosf, sinf = _rope_tables(seq)
    qkv = _rope(qkv, cosf, sinf)
    kv_c = qkv[:, NSA_WIDTH:NSA_WIDTH + 2 * KV_WIDTH]
    tok = kv_c.reshape(nc, CMP_STRIDE, 2, NSA_GROUPS, HEAD_DIM)
    tok = jnp.transpose(tok, (2, 3, 0, 1, 4)).reshape(2, NSA_GROUPS, nc, CMP_STRIDE * HEAD_DIM)
    tok_next = jnp.concatenate([tok[:, :, 1:], jnp.zeros_like(tok[:, :, :1])], axis=2)
    pe = jnp.stack([cmp_k_pe, cmp_v_pe]).reshape(2, 1, CMP_BLOCK * HEAD_DIM)
    kv_cmp = _compress(tok, tok_next, pe, jnp.stack([cmp_k_w1, cmp_v_w1]), jnp.stack([cmp_k_w2, cmp_v_w2]))
    o_cmp, sel = _cmp_attention(qkv, kv_cmp, _overlap_matrix(nc), min(SEL_TOPK, n_sel))
    o_slc = _slc_attention(qkv, sel)
    o_win = _win_attention(qkv)

    mu = _split_rwkv_cols(rwkv_mu.reshape(1, -1))
    r, lw, k, v, kk, a, g = _rwkv_prep(
        z, mu, rwkv_w0.reshape(1, w), _pad_rows(rwkv_w_up, LORA_PAD), rwkv_a0.reshape(1, w),
        _pad_rows(rwkv_a_up, LORA_PAD), rwkv_g_up, rwkv_k_k.reshape(1, w), rwkv_k_a.reshape(1, w))
    o_b = _rwkv_scan(r, lw, k, v, kk, a, g, rwkv_r_k.reshape(1, w), rwkv_ln_w.reshape(1, w),
                     rwkv_ln_b.reshape(1, w))

    mixed = _merge(o_cmp, o_slc, o_win, bgate, o_b, mgate, proj_nsa.astype(BF16), proj_rwkv.astype(BF16))
    x1, h2 = _outproj(x, mixed, w_out.astype(BF16), norm2_g)
    act = _ffn_up(h2, ffn_up.astype(BF16), ffn_conv_w, ffn_conv_b.reshape(1, -1))
    return _ffn_down(act, ffn_down.astype(BF16), x1, final_g)


def kernel(x, norm1_g, w_in, cmp_k_pe, cmp_k_w1, cmp_k_w2, cmp_v_pe, cmp_v_w1, cmp_v_w2, rwkv_mu, rwkv_w0,
           rwkv_w_up, rwkv_a0, rwkv_a_up, rwkv_g_up, rwkv_k_k, rwkv_k_a, rwkv_r_k, rwkv_ln_w, rwkv_ln_b,
           proj_nsa, proj_rwkv, w_out, norm2_g, ffn_up, ffn_conv_w, ffn_conv_b, ffn_down, final_g):
    assert norm1_g.shape[0] == 1, "single-layer block: the final rmsnorm is fused into the down projection"
    layer = (norm1_g, w_in, cmp_k_pe, cmp_k_w1, cmp_k_w2, cmp_v_pe, cmp_v_w1, cmp_v_w2, rwkv_mu, rwkv_w0,
             rwkv_w_up, rwkv_a0, rwkv_a_up, rwkv_g_up, rwkv_k_k, rwkv_k_a, rwkv_r_k, rwkv_ln_w, rwkv_ln_b,
             proj_nsa, proj_rwkv, w_out, norm2_g, ffn_up, ffn_conv_w, ffn_conv_b, ffn_down)
    weights = [t[0] for t in layer]
    return jnp.stack([_layer(x[b], *weights, final_g) for b in range(x.shape[0])])
```

```python
import functools

import jax
import jax.numpy as jnp
from jax import lax
from jax.experimental import pallas as pl
from jax.experimental.pallas import tpu as pltpu

F32 = jnp.float32
BF16 = jnp.bfloat16
HI = lax.Precision.HIGHEST

LANES = 128
D_MODEL = 2048
HEAD_DIM = 128
NSA_WIDTH = D_MODEL // 2
NSA_HEADS = NSA_WIDTH // HEAD_DIM
NSA_GROUPS = 2
HPG = NSA_HEADS // NSA_GROUPS
KV_WIDTH = NSA_GROUPS * HEAD_DIM
NSA_BRANCHES = 3
CMP_BLOCK = 32
CMP_STRIDE = 16
SEL_BLOCK = 64
SEL_TOPK = 16
N_LOCAL_SEL = 2
WINDOW = 512
Q_BLOCK = 128
ROPE_THETA = 10000.0
ATTN_SCALE = HEAD_DIM ** -0.5
QUERY_SCALE = ATTN_SCALE * 1.4426950408889634
FORCE_BONUS = 1.0e4
NEG = -1.0e30
PICKED = -3.0e38

RWKV_WIDTH = D_MODEL // 2
RWKV_HEAD = 64
DECAY_LORA = 96
ICLR_LORA = 96
GATE_LORA = 256
LORA_PAD = 128
RWKV_Z = 3 * RWKV_WIDTH + 2 * LORA_PAD + GATE_LORA
GN_EPS = 64e-5
SCAN_CHUNK = 64

D_FF = 5632
CONV_WIDTH = 3
EPS = 1e-6

QKV_WIDTH = NSA_WIDTH + 6 * KV_WIDTH
QKV_BLOCKS = QKV_WIDTH // LANES
ROPE_BLOCKS = tuple(range(0, 10)) + (12, 13, 16, 17)
KS_BLOCK, VS_BLOCK, KW_BLOCK, VW_BLOCK = 12, 14, 16, 18

VMEM_LIMIT = 56 * 1024 * 1024


def _params(n_axes):
    return pltpu.CompilerParams(dimension_semantics=("arbitrary",) * n_axes,
                                vmem_limit_bytes=VMEM_LIMIT)


def _dot(a, b, precision=None):
    return jnp.dot(a, b, preferred_element_type=F32, precision=precision)


def _dot_nt(a, b, precision=None):
    return lax.dot_general(a, b, (((1,), (1,)), ((), ())), preferred_element_type=F32,
                           precision=precision)


def _dot_tn(a, b, precision=None):
    return lax.dot_general(a, b, (((0,), (0,)), ((), ())), preferred_element_type=F32,
                           precision=precision)


def _iota(shape, dim):
    return lax.broadcasted_iota(jnp.int32, shape, dim)


def _rmsnorm_kernel(x_ref, g_ref, o_ref):
    x = x_ref[...]
    y = x * lax.rsqrt(jnp.mean(x * x, axis=-1, keepdims=True) + EPS)
    o_ref[...] = (y * g_ref[...]).astype(o_ref.dtype)


def _rmsnorm(x, g, out_dtype, tm=512):
    s, d = x.shape
    return pl.pallas_call(
        _rmsnorm_kernel, grid=(s // tm,),
        in_specs=[pl.BlockSpec((tm, d), lambda i: (i, 0)), pl.BlockSpec((1, d), lambda i: (0, 0))],
        out_specs=pl.BlockSpec((tm, d), lambda i: (i, 0)),
        out_shape=jax.ShapeDtypeStruct((s, d), out_dtype),
        compiler_params=_params(1), name="rmsnorm")(x, g.reshape(1, d))


def _mm_kernel(a_ref, b_ref, o_ref):
    o_ref[...] = _dot(a_ref[...], b_ref[...]).astype(o_ref.dtype)


def _matmul(a, b, out_dtype, tm, tn, name):
    m, k = a.shape
    n = b.shape[1]
    return pl.pallas_call(
        _mm_kernel, grid=(m // tm, n // tn),
        in_specs=[pl.BlockSpec((tm, k), lambda i, j: (i, 0)), pl.BlockSpec((k, tn), lambda i, j: (0, j))],
        out_specs=pl.BlockSpec((tm, tn), lambda i, j: (i, j)),
        out_shape=jax.ShapeDtypeStruct((m, n), out_dtype),
        compiler_params=_params(2), name=name)(a, b)


def _rope_kernel(p_ref, cos_ref, sin_ref, o_ref):
    c = cos_ref[...]
    s = sin_ref[...]
    for b in range(QKV_BLOCKS):
        t = p_ref[:, b * LANES:(b + 1) * LANES]
        if b in ROPE_BLOCKS:
            t = t * c + pltpu.roll(t, HEAD_DIM // 2, 1) * s
        if b < NSA_HEADS:
            t = t * QUERY_SCALE
        o_ref[:, b * LANES:(b + 1) * LANES] = t.astype(o_ref.dtype)


def _rope(qkv, cosf, sinf, tm=256):
    s = qkv.shape[0]
    return pl.pallas_call(
        _rope_kernel, grid=(s // tm,),
        in_specs=[pl.BlockSpec((tm, QKV_WIDTH), lambda i: (i, 0)),
                  pl.BlockSpec((tm, HEAD_DIM), lambda i: (i, 0)),
                  pl.BlockSpec((tm, HEAD_DIM), lambda i: (i, 0))],
        out_specs=pl.BlockSpec((tm, QKV_WIDTH), lambda i: (i, 0)),
        out_shape=jax.ShapeDtypeStruct((s, QKV_WIDTH), BF16),
        compiler_params=_params(1), name="rope")(qkv, cosf, sinf)


def _compress_kernel(ta_ref, tb_ref, pe_ref, w1_ref, w2_ref, o_ref):
    half = CMP_STRIDE * HEAD_DIM
    w1 = w1_ref[0]
    w1b = w1.astype(BF16)
    pre = _dot(ta_ref[0, 0], w1b[:half]) + _dot(tb_ref[0, 0], w1b[half:])
    pe = jnp.broadcast_to(pe_ref[0], (8, CMP_BLOCK * HEAD_DIM))
    pre = pre + _dot(pe, w1, HI)[0:1]
    h = jax.nn.gelu(pre)
    o_ref[0, 0] = _dot(h.astype(BF16), w2_ref[0].astype(BF16))


def _compress(ta, tb, pe, w1, w2):
    _, g, nc, width = ta.shape
    tok = pl.BlockSpec((1, 1, nc, width), lambda k, gi: (k, gi, 0, 0))
    return pl.pallas_call(
        _compress_kernel, grid=(2, g),
        in_specs=[tok, tok,
                  pl.BlockSpec((1, 1, CMP_BLOCK * HEAD_DIM), lambda k, gi: (k, 0, 0)),
                  pl.BlockSpec((1, CMP_BLOCK * HEAD_DIM, HEAD_DIM), lambda k, gi: (k, 0, 0)),
                  pl.BlockSpec((1, HEAD_DIM, HEAD_DIM), lambda k, gi: (k, 0, 0))],
        out_specs=pl.BlockSpec((1, 1, nc, HEAD_DIM), lambda k, gi: (k, gi, 0, 0)),
        out_shape=jax.ShapeDtypeStruct((2, g, nc, HEAD_DIM), F32),
        compiler_params=_params(2), name="compress")(ta, tb, pe, w1, w2)


def _stack_heads(q_ref):
    return jnp.concatenate([q_ref[:, j * HEAD_DIM:(j + 1) * HEAD_DIM] for j in range(HPG)], axis=0)


def _store_heads(o_ref, o):
    for j in range(HPG):
        o_ref[:, j * HEAD_DIM:(j + 1) * HEAD_DIM] = o[j * Q_BLOCK:(j + 1) * Q_BLOCK].astype(o_ref.dtype)


def _cmp_kernel(q_ref, kc_ref, vc_ref, ov_ref, o_ref, sel_ref, *, nc, top):
    i = pl.program_id(1)
    kc = kc_ref[0, 0].astype(BF16)
    vc = vc_ref[0, 0].astype(BF16)
    t = i * Q_BLOCK + _iota((Q_BLOCK, 1), 0)
    c_end = _iota((1, nc), 1) * CMP_STRIDE + (CMP_BLOCK - 1)
    cmask = (c_end <= t)[None]
    s = _dot_nt(_stack_heads(q_ref), kc).reshape(HPG, Q_BLOCK, nc)
    s = jnp.where(cmask, s, NEG)
    e = jnp.exp2(s - jnp.max(s, axis=-1, keepdims=True))
    p = e / jnp.sum(e, axis=-1, keepdims=True)
    p = jnp.where(cmask, p, 0.0)
    _store_heads(o_ref, _dot(p.reshape(HPG * Q_BLOCK, nc).astype(BF16), vc))
    imp = _dot(jnp.sum(p, axis=0), ov_ref[...], HI)

    jb = _iota((1, LANES), 1)
    cur = jnp.right_shift(t, 6)
    forced = (jb == 0) | ((jb <= cur) & (jb > cur - N_LOCAL_SEL))
    visible = jb * SEL_BLOCK <= t
    val = jnp.where(visible, imp + FORCE_BONUS * forced.astype(F32), NEG)
    jbf = jb.astype(F32)
    sel = jnp.zeros((Q_BLOCK, LANES), F32)
    for _ in range(top):
        mx = jnp.max(val, axis=-1, keepdims=True)
        idx = jnp.min(jnp.where(val == mx, jbf, float(LANES)), axis=-1, keepdims=True)
        pick = jbf == idx
        sel = jnp.where(pick, 1.0, sel)
        val = jnp.where(pick, PICKED, val)
    sel_ref[...] = jnp.where(visible, sel, 0.0)


def _cmp_attention(qkv, kvc, overlap, top):
    s = qkv.shape[0]
    nc = kvc.shape[2]
    nqb = s // Q_BLOCK
    gw = HPG * HEAD_DIM
    return pl.pallas_call(
        functools.partial(_cmp_kernel, nc=nc, top=top), grid=(NSA_GROUPS, nqb),
        in_specs=[pl.BlockSpec((Q_BLOCK, gw), lambda g, i: (i, g)),
                  pl.BlockSpec((1, 1, nc, HEAD_DIM), lambda g, i: (0, g, 0, 0)),
                  pl.BlockSpec((1, 1, nc, HEAD_DIM), lambda g, i: (1, g, 0, 0)),
                  pl.BlockSpec((nc, LANES), lambda g, i: (0, 0))],
        out_specs=[pl.BlockSpec((Q_BLOCK, gw), lambda g, i: (i, g)),
                   pl.BlockSpec((Q_BLOCK, LANES), lambda g, i: (i, g))],
        out_shape=[jax.ShapeDtypeStruct((s, NSA_WIDTH), F32),
                   jax.ShapeDtypeStruct((s, NSA_GROUPS * LANES), F32)],
        compiler_params=_params(2), name="cmp_attn")(qkv, kvc, kvc, overlap)


def _slc_kernel(q_ref, k_ref, v_ref, sel_ref, o_ref, m_ref, l_ref, acc_ref, *, tk):
    i = pl.program_id(1)
    rows = HPG * Q_BLOCK
    q = _stack_heads(q_ref)
    sel = sel_ref[...].astype(BF16)
    qpos = i * Q_BLOCK + _iota((Q_BLOCK, 1), 0)
    blk_off = _iota((LANES, tk), 0) - jnp.right_shift(_iota((LANES, tk), 1), 6)
    blocks_per_tile = tk // SEL_BLOCK
    m_ref[...] = jnp.full(m_ref.shape, NEG, F32)
    l_ref[...] = jnp.zeros(l_ref.shape, F32)
    acc_ref[...] = jnp.zeros(acc_ref.shape, F32)
    last = lax.div(i * Q_BLOCK + Q_BLOCK - 1, tk)

    def tile(kt, causal):
        start = pl.multiple_of(kt * tk, tk)
        expand = jnp.where(blk_off == kt * blocks_per_tile, 1.0, 0.0).astype(BF16)
        chosen = _dot(sel, expand)
        if causal:
            chosen = jnp.where(start + _iota((1, tk), 1) <= qpos, chosen, 0.0)
        s = _dot_nt(q, k_ref[pl.ds(start, tk), :]).reshape(HPG, Q_BLOCK, tk)
        s = jnp.where((chosen > 0.5)[None], s, NEG)
        cols = [s[:, :, c * LANES:(c + 1) * LANES] for c in range(tk // LANES)]
        m_old = m_ref[...]
        m_new = jnp.maximum(m_old, jnp.max(functools.reduce(jnp.maximum, cols), axis=-1, keepdims=True))
        alpha = jnp.exp2(m_old - m_new)
        ps = [jnp.exp2(col - m_new) for col in cols]
        m_ref[...] = m_new
        l_ref[...] = alpha * l_ref[...] + functools.reduce(jnp.add, ps)
        p = jnp.concatenate(ps, axis=-1).reshape(rows, tk).astype(BF16)
        acc_ref[...] = alpha.reshape(rows, LANES) * acc_ref[...] + _dot(p, v_ref[pl.ds(start, tk), :])

    def body(j, carry):
        tile(2 * j, False)
        tile(2 * j + 1, False)
        return carry

    lax.fori_loop(0, lax.div(last, 2), body, 0)

    @pl.when(lax.rem(last, 2) == 1)
    def _():
        tile(last - 1, False)

    tile(last, True)
    m = m_ref[...].reshape(rows, LANES)
    l = jnp.sum(l_ref[...], axis=-1, keepdims=True).reshape(rows, 1)
    _store_heads(o_ref, jnp.where(m > 0.5 * NEG, acc_ref[...] / l, 0.0))


def _slc_attention(qkv, sel, tk=512):
    s = qkv.shape[0]
    gw = HPG * HEAD_DIM
    return pl.pallas_call(
        functools.partial(_slc_kernel, tk=tk), grid=(NSA_GROUPS, s // Q_BLOCK),
        in_specs=[pl.BlockSpec((Q_BLOCK, gw), lambda g, i: (i, g)),
                  pl.BlockSpec((s, HEAD_DIM), lambda g, i: (0, KS_BLOCK + g)),
                  pl.BlockSpec((s, HEAD_DIM), lambda g, i: (0, VS_BLOCK + g)),
                  pl.BlockSpec((Q_BLOCK, LANES), lambda g, i: (i, g))],
        out_specs=pl.BlockSpec((Q_BLOCK, gw), lambda g, i: (i, g)),
        out_shape=jax.ShapeDtypeStruct((s, NSA_WIDTH), F32),
        scratch_shapes=[pltpu.VMEM((HPG, Q_BLOCK, LANES), F32), pltpu.VMEM((HPG, Q_BLOCK, LANES), F32),
                        pltpu.VMEM((HPG * Q_BLOCK, HEAD_DIM), F32)],
        compiler_params=_params(2), name="slc_attn")(qkv, qkv, qkv, sel)


def _win_kernel(q_ref, k_ref, v_ref, o_ref):
    i = pl.program_id(1)
    band = WINDOW + Q_BLOCK
    start = pl.multiple_of(jnp.maximum(i * Q_BLOCK - WINDOW, 0), Q_BLOCK)
    k_b = k_ref[pl.ds(start, band), :]
    v_b = v_ref[pl.ds(start, band), :]
    qpos = i * Q_BLOCK + _iota((Q_BLOCK, 1), 0)
    kpos = start + _iota((1, band), 1)
    allowed = ((kpos <= qpos) & (kpos > qpos - WINDOW))[None]
    s = _dot_nt(_stack_heads(q_ref), k_b).reshape(HPG, Q_BLOCK, band)
    s = jnp.where(allowed, s, NEG)
    e = jnp.exp2(s - jnp.max(s, axis=-1, keepdims=True))
    p = e / jnp.sum(e, axis=-1, keepdims=True)
    _store_heads(o_ref, _dot(p.reshape(HPG * Q_BLOCK, band).astype(BF16), v_b))


def _win_attention(qkv):
    s = qkv.shape[0]
    gw = HPG * HEAD_DIM
    return pl.pallas_call(
        _win_kernel, grid=(NSA_GROUPS, s // Q_BLOCK),
        in_specs=[pl.BlockSpec((Q_BLOCK, gw), lambda g, i: (i, g)),
                  pl.BlockSpec((s, HEAD_DIM), lambda g, i: (0, KW_BLOCK + g)),
                  pl.BlockSpec((s, HEAD_DIM), lambda g, i: (0, VW_BLOCK + g))],
        out_specs=pl.BlockSpec((Q_BLOCK, gw), lambda g, i: (i, g)),
        out_shape=jax.ShapeDtypeStruct((s, NSA_WIDTH), F32),
        compiler_params=_params(2), name="win_attn")(qkv, qkv, qkv)


def _head_sum(x):
    seg = (jnp.right_shift(_iota((LANES, LANES), 0), 6) ==
           jnp.right_shift(_iota((LANES, LANES), 1), 6)).astype(F32)
    return _dot(x, seg, HI)


def _rwkv_prep_kernel(z_ref, zp_ref, mu_ref, w0_ref, wup_ref, a0_ref, aup_ref, gup_ref, kscale_ref, ka_ref,
                      r_out, lw_out, k_out, v_out, kk_out, a_out, g_out):
    i = pl.program_id(0)
    z = z_ref[...]
    tm = z.shape[0]
    last = jnp.where(i == 0, 0.0, zp_ref[7:8, :])
    z_prev = jnp.where(_iota((tm, 1), 0) == 0, last, pltpu.roll(z, 1, 0))
    z = z + (z_prev - z) * mu_ref[...]
    w = RWKV_WIDTH
    r, k, v = z[:, 0:w], z[:, w:2 * w], z[:, 2 * w:3 * w]
    wd = z[:, 3 * w:3 * w + LORA_PAD]
    ad = z[:, 3 * w + LORA_PAD:3 * w + 2 * LORA_PAD]
    gd = z[:, 3 * w + 2 * LORA_PAD:]
    wlog = -jax.nn.softplus(-(w0_ref[...] + _dot(jnp.tanh(wd), wup_ref[...], HI))) - 0.5
    a = jax.nn.sigmoid(a0_ref[...] + _dot(ad, aup_ref[...], HI))
    g = _dot(jax.nn.sigmoid(gd), gup_ref[...], HI)
    kk = k * kscale_ref[...]
    for b in range(w // LANES):
        sl = slice(b * LANES, (b + 1) * LANES)
        kb = kk[:, sl]
        kk_out[:, sl] = kb * lax.rsqrt(jnp.maximum(_head_sum(kb * kb), 1e-24))
    r_out[...] = r
    lw_out[...] = -jnp.exp(wlog)
    k_out[...] = k * (1.0 + (a - 1.0) * ka_ref[...])
    v_out[...] = v
    a_out[...] = a
    g_out[...] = g


def _rwkv_prep(z, mu, w0, w_up, a0, a_up, g_up, k_k, k_a, tm=256):
    s = z.shape[0]
    w = RWKV_WIDTH
    row = lambda n: pl.BlockSpec((1, n), lambda i: (0, 0))
    full = lambda a: pl.BlockSpec(a.shape, lambda i: (0, 0))
    out = pl.BlockSpec((tm, w), lambda i: (i, 0))
    return pl.pallas_call(
        _rwkv_prep_kernel, grid=(s // tm,),
        in_specs=[pl.BlockSpec((tm, RWKV_Z), lambda i: (i, 0)),
                  pl.BlockSpec((8, RWKV_Z), lambda i: (jnp.maximum(i * (tm // 8) - 1, 0), 0)),
                  row(RWKV_Z), row(w), full(w_up), row(w), full(a_up), full(g_up), row(w), row(w)],
        out_specs=[out] * 7,
        out_shape=[jax.ShapeDtypeStruct((s, w), F32)] * 7,
        compiler_params=_params(1), name="rwkv_prep")(z, z, mu, w0, w_up, a0, a_up, g_up, k_k, k_a)


def _scan_kernel(r_ref, lw_ref, k_ref, v_ref, kk_ref, a_ref, g_ref, rk_ref, lnw_ref, lnb_ref,
                 o_ref, state_ref):
    c = SCAN_CHUNK

    @pl.when(pl.program_id(0) == 0)
    def _():
        state_ref[...] = jnp.zeros_like(state_ref)

    tri = (_iota((c, c), 0) >= _iota((c, c), 1)).astype(F32)
    head0 = _iota((1, LANES), 1) < RWKV_HEAD
    tok_r = jnp.bitwise_and(_iota((LANES, LANES), 0), c - 1)
    tok_c = jnp.bitwise_and(_iota((LANES, LANES), 1), c - 1)
    strict = tok_r > tok_c
    incl = tok_r >= tok_c
    eye = (_iota((LANES, LANES), 0) == _iota((LANES, LANES), 1)).astype(F32)

    def stack(x):
        return jnp.concatenate([jnp.where(head0, x, 0.0), jnp.where(head0, 0.0, x)], axis=0)

    pairs = range(RWKV_WIDTH // LANES)
    sls = [slice(p * LANES, (p + 1) * LANES) for p in pairs]
    lw = [lw_ref[:, sl] for sl in sls]
    cum = [_dot(tri, x, HI) for x in lw]
    total = [x[c - 1:c, :] for x in cum]
    r = [r_ref[:, sl] for sl in sls]
    k = [k_ref[:, sl] for sl in sls]
    v = [v_ref[:, sl] for sl in sls]
    kk = [kk_ref[:, sl] for sl in sls]
    b = [kk[p] * a_ref[:, sls[p]] for p in pairs]
    e_neg = [jnp.exp(-x) for x in cum]
    a2b = [stack(-kk[p] * jnp.exp(cum[p] - lw[p])).astype(BF16) for p in pairs]
    r2b = [stack(r[p] * jnp.exp(cum[p])).astype(BF16) for p in pairs]
    b2 = [stack(b[p] * e_neg[p]) for p in pairs]
    k2 = [stack(k[p] * e_neg[p]) for p in pairs]
    v2 = [stack(x).astype(BF16) for x in v]
    gram = [_dot_nt(jnp.concatenate([a2b[p], r2b[p]], axis=0),
                    jnp.concatenate([b2[p], k2[p]], axis=0).astype(BF16)) for p in pairs]
    l_ab = [jnp.where(strict, x[0:LANES, 0:LANES], 0.0) for x in gram]
    m_ak = [jnp.where(strict, x[0:LANES, LANES:], 0.0).astype(BF16) for x in gram]
    m_rb = [jnp.where(incl, x[LANES:, 0:LANES], 0.0).astype(BF16) for x in gram]
    m_rk = [jnp.where(incl, x[LANES:, LANES:], 0.0).astype(BF16) for x in gram]
    inv = [eye + x for x in l_ab]
    pw = l_ab
    for _ in range(5):
        pwb = [x.astype(BF16) for x in pw]
        pw = [_dot(x, x) for x in pwb]
        inv = [inv[p] + _dot(inv[p].astype(BF16), pw[p].astype(BF16)) for p in pairs]
    state = [state_ref[p] for p in pairs]
    sb = [x.astype(BF16) for x in state]
    x = [_dot_nt(a2b[p], sb[p]) + _dot(m_ak[p], v2[p]) for p in pairs]
    u2 = [_dot(inv[p].astype(BF16), x[p].astype(BF16)).astype(BF16) for p in pairs]
    y2 = [_dot_nt(r2b[p], sb[p]) + _dot(m_rb[p], u2[p]) + _dot(m_rk[p], v2[p]) for p in pairs]
    for p in pairs:
        e_end = jnp.exp(total[p] - cum[p])
        uv = jnp.concatenate([u2[p], v2[p]], axis=0)
        bk = jnp.concatenate([stack(b[p] * e_end), stack(k[p] * e_end)], axis=0).astype(BF16)
        state_ref[p] = state[p] * jnp.exp(total[p]) + _dot_tn(uv, bk)
    y = [x[0:c] + x[c:] for x in y2]
    mean = [_head_sum(x) * (1.0 / RWKV_HEAD) for x in y]
    d = [y[p] - mean[p] for p in pairs]
    var = [_head_sum(x * x) * (1.0 / RWKV_HEAD) for x in d]
    bonus = [_head_sum(r[p] * k[p] * rk_ref[:, sls[p]]) for p in pairs]
    for p in pairs:
        sl = sls[p]
        out = d[p] * lax.rsqrt(var[p] + GN_EPS) * lnw_ref[:, sl] + lnb_ref[:, sl]
        o_ref[:, sl] = (out + bonus[p] * v[p]) * g_ref[:, sl]


def _rwkv_scan(r, lw, k, v, kk, a, g, r_k, ln_w, ln_b):
    s, w = r.shape
    blk = pl.BlockSpec((SCAN_CHUNK, w), lambda c: (c, 0))
    row = pl.BlockSpec((1, w), lambda c: (0, 0))
    return pl.pallas_call(
        _scan_kernel, grid=(s // SCAN_CHUNK,),
        in_specs=[blk] * 7 + [row] * 3,
        out_specs=blk,
        out_shape=jax.ShapeDtypeStruct((s, w), F32),
        scratch_shapes=[pltpu.VMEM((w // LANES, LANES, LANES), F32)],
        compiler_params=_params(1), name="rwkv_scan")(r, lw, k, v, kk, a, g, r_k, ln_w, ln_b)


def _merge_kernel(oc_ref, os_ref, ow_ref, bg_ref, ob_ref, mg_ref, pa_ref, pb_ref, o_ref):
    bg = jax.nn.sigmoid(bg_ref[...])
    heads = []
    for h in range(NSA_HEADS):
        sl = slice(h * HEAD_DIM, (h + 1) * HEAD_DIM)
        c0 = NSA_BRANCHES * h
        heads.append(bg[:, c0:c0 + 1] * oc_ref[:, sl] + bg[:, c0 + 1:c0 + 2] * os_ref[:, sl]
                     + bg[:, c0 + 2:c0 + 3] * ow_ref[:, sl])
    o_a = jnp.concatenate(heads, axis=-1).astype(BF16)
    y_a = _dot(o_a, pa_ref[...])
    y_b = _dot(ob_ref[...].astype(BF16), pb_ref[...])
    mg = jax.nn.sigmoid(mg_ref[...])
    o_ref[...] = (mg[:, :D_MODEL] * y_a + mg[:, D_MODEL:] * y_b).astype(o_ref.dtype)


def _merge(o_cmp, o_slc, o_win, bgate, o_b, mgate, proj_a, proj_b, tm=256):
    s = o_cmp.shape[0]
    half = lambda: pl.BlockSpec((tm, NSA_WIDTH), lambda i: (i, 0))
    wspec = pl.BlockSpec((NSA_WIDTH, D_MODEL), lambda i: (0, 0))
    return pl.pallas_call(
        _merge_kernel, grid=(s // tm,),
        in_specs=[half(), half(), half(), pl.BlockSpec((tm, LANES), lambda i: (i, 0)), half(),
                  pl.BlockSpec((tm, 2 * D_MODEL), lambda i: (i, 0)), wspec, wspec],
        out_specs=pl.BlockSpec((tm, D_MODEL), lambda i: (i, 0)),
        out_shape=jax.ShapeDtypeStruct((s, D_MODEL), BF16),
        compiler_params=_params(1), name="merge")(o_cmp, o_slc, o_win, bgate, o_b, mgate, proj_a, proj_b)


def _outproj_kernel(x_ref, m_ref, w_ref, g_ref, x1_ref, h_ref):
    x1 = x_ref[...] + _dot(m_ref[...], w_ref[...])
    x1_ref[...] = x1
    y = x1 * lax.rsqrt(jnp.mean(x1 * x1, axis=-1, keepdims=True) + EPS)
    h_ref[...] = (y * g_ref[...]).astype(h_ref.dtype)


def _outproj(x, mixed, w_out, g, tm=256):
    s, d = x.shape
    blk = pl.BlockSpec((tm, d), lambda i: (i, 0))
    return pl.pallas_call(
        _outproj_kernel, grid=(s // tm,),
        in_specs=[blk, blk, pl.BlockSpec((d, d), lambda i: (0, 0)), pl.BlockSpec((1, d), lambda i: (0, 0))],
        out_specs=[blk, blk],
        out_shape=[jax.ShapeDtypeStruct((s, d), F32), jax.ShapeDtypeStruct((s, d), BF16)],
        compiler_params=_params(1), name="outproj")(x, mixed, w_out, g.reshape(1, d))


def _ffn_up_kernel(h_ref, wg_ref, wv_ref, cwg_ref, cwv_ref, cbg_ref, cbv_ref, o_ref, ug_ref, uv_ref):
    tm = h_ref.shape[0]

    @pl.when(pl.program_id(1) == 0)
    def _():
        ug_ref[0:8, :] = jnp.zeros((8, ug_ref.shape[1]), F32)
        uv_ref[0:8, :] = jnp.zeros((8, uv_ref.shape[1]), F32)

    h = h_ref[...]

    def conv(u_ref, w_ref, cw_ref, cb_ref):
        u_ref[8:8 + tm, :] = _dot(h, w_ref[...])
        out = cb_ref[...]
        for tap in range(CONV_WIDTH):
            off = 8 - (CONV_WIDTH - 1 - tap)
            out = out + u_ref[off:off + tm, :] * cw_ref[tap:tap + 1, :]
        u_ref[0:8, :] = u_ref[tm:tm + 8, :]
        return out

    gate = conv(ug_ref, wg_ref, cwg_ref, cbg_ref)
    val = conv(uv_ref, wv_ref, cwv_ref, cbv_ref)
    o_ref[...] = (jax.nn.silu(gate) * val).astype(o_ref.dtype)


def _ffn_up(h, w_up, conv_w, conv_b, tm=512, tn=512):
    s, d = h.shape
    nj = D_FF // tn
    return pl.pallas_call(
        _ffn_up_kernel, grid=(nj, s // tm),
        in_specs=[pl.BlockSpec((tm, d), lambda j, i: (i, 0)),
                  pl.BlockSpec((d, tn), lambda j, i: (0, j)),
                  pl.BlockSpec((d, tn), lambda j, i: (0, nj + j)),
                  pl.BlockSpec((CONV_WIDTH, tn), lambda j, i: (0, j)),
                  pl.BlockSpec((CONV_WIDTH, tn), lambda j, i: (0, nj + j)),
                  pl.BlockSpec((1, tn), lambda j, i: (0, j)),
                  pl.BlockSpec((1, tn), lambda j, i: (0, nj + j))],
        out_specs=pl.BlockSpec((tm, tn), lambda j, i: (i, j)),
        out_shape=jax.ShapeDtypeStruct((s, D_FF), BF16),
        scratch_shapes=[pltpu.VMEM((tm + 8, tn), F32), pltpu.VMEM((tm + 8, tn), F32)],
        compiler_params=_params(2), name="ffn_up")(h, w_up, w_up, conv_w, conv_w, conv_b, conv_b)


def _ffn_down_kernel(a_ref, w_ref, x_ref, g_ref, o_ref, acc_ref):
    kk = pl.program_id(1)

    @pl.when(kk == 0)
    def _():
        acc_ref[...] = x_ref[...]

    acc_ref[...] += _dot(a_ref[...], w_ref[...])

    @pl.when(kk == pl.num_programs(1) - 1)
    def _():
        x2 = acc_ref[...]
        y = x2 * lax.rsqrt(jnp.mean(x2 * x2, axis=-1, keepdims=True) + EPS)
        o_ref[...] = y * g_ref[...]


def _ffn_down(act, w_down, x1, g, tm=512, tk=512):
    s, d = x1.shape
    return pl.pallas_call(
        _ffn_down_kernel, grid=(s // tm, D_FF // tk),
        in_specs=[pl.BlockSpec((tm, tk), lambda i, k: (i, k)),
                  pl.BlockSpec((tk, d), lambda i, k: (k, 0)),
                  pl.BlockSpec((tm, d), lambda i, k: (i, 0)),
                  pl.BlockSpec((1, d), lambda i, k: (0, 0))],
        out_specs=pl.BlockSpec((tm, d), lambda i, k: (i, 0)),
        out_shape=jax.ShapeDtypeStruct((s, d), F32),
        scratch_shapes=[pltpu.VMEM((tm, d), F32)],
        compiler_params=_params(2), name="ffn_down")(act, w_down, x1, g.reshape(1, d))


def _pad_cols(w, n):
    return jnp.pad(w, ((0, 0), (0, n - w.shape[1])))


def _pad_rows(w, n):
    return jnp.pad(w, ((0, n - w.shape[0]), (0, 0)))


def _split_rwkv_cols(t):
    w = RWKV_WIDTH
    rkv, wd, ad, gd = (t[:, :3 * w], t[:, 3 * w:3 * w + DECAY_LORA],
                       t[:, 3 * w + DECAY_LORA:3 * w + DECAY_LORA + ICLR_LORA],
                       t[:, 3 * w + DECAY_LORA + ICLR_LORA:])
    return jnp.concatenate([rkv, _pad_cols(wd, LORA_PAD), _pad_cols(ad, LORA_PAD), gd], axis=1)


def _rope_tables(seq):
    half = HEAD_DIM // 2
    inv = ROPE_THETA ** (-jnp.arange(half, dtype=F32) / half)
    ang = jnp.arange(seq, dtype=F32)[:, None] * inv[None, :]
    cos, sin = jnp.cos(ang), jnp.sin(ang)
    return jnp.concatenate([cos, cos], axis=1), jnp.concatenate([-sin, sin], axis=1)


def _overlap_matrix(nc):
    c_start = jnp.arange(nc) * CMP_STRIDE
    s_start = jnp.arange(LANES) * SEL_BLOCK
    ov = jnp.clip(jnp.minimum(c_start[:, None] + CMP_BLOCK, s_start[None, :] + SEL_BLOCK)
                  - jnp.maximum(c_start[:, None], s_start[None, :]), 0)
    return ov.astype(F32) / CMP_BLOCK


def _layer(x, norm1_g, w_in, cmp_k_pe, cmp_k_w1, cmp_k_w2, cmp_v_pe, cmp_v_w1, cmp_v_w2, rwkv_mu, rwkv_w0,
           rwkv_w_up, rwkv_a0, rwkv_a_up, rwkv_g_up, rwkv_k_k, rwkv_k_a, rwkv_r_k, rwkv_ln_w, rwkv_ln_b,
           proj_nsa, proj_rwkv, w_out, norm2_g, ffn_up, ffn_conv_w, ffn_conv_b, ffn_down, final_g):
    seq = x.shape[0]
    n_sel = seq // SEL_BLOCK
    assert seq % 2048 == 0 and n_sel <= LANES
    nc = seq // CMP_STRIDE
    w = RWKV_WIDTH

    c_gate = QKV_WIDTH
    c_z = c_gate + NSA_HEADS * NSA_BRANCHES
    c_merge = c_z + 3 * w + DECAY_LORA + ICLR_LORA + GATE_LORA
    w_qkv = w_in[:, :c_gate].astype(BF16)
    w_bgate = _pad_cols(w_in[:, c_gate:c_z], LANES).astype(BF16)
    w_z = _split_rwkv_cols(w_in[:, c_z:c_merge]).astype(BF16)
    w_merge = w_in[:, c_merge:].astype(BF16)

    h = _rmsnorm(x, norm1_g, BF16)
    qkv = _matmul(h, w_qkv, F32, 512, 1280, "proj_qkv")
    bgate = _matmul(h, w_bgate, F32, 512, LANES, "proj_bgate")
    z = _matmul(h, w_z, F32, 512, 896, "proj_rwkv")
    mgate = _matmul(h, w_merge, F32, 512, 1024, "proj_merge")

    cosf, sinf = _rope_tables(seq)
    qkv = _rope(qkv, cosf, sinf)
    kv_c = qkv[:, NSA_WIDTH:NSA_WIDTH + 2 * KV_WIDTH]
    tok = kv_c.reshape(nc, CMP_STRIDE, 2, NSA_GROUPS, HEAD_DIM)
    tok = jnp.transpose(tok, (2, 3, 0, 1, 4)).reshape(2, NSA_GROUPS, nc, CMP_STRIDE * HEAD_DIM)
    tok_next = jnp.concatenate([tok[:, :, 1:], jnp.zeros_like(tok[:, :, :1])], axis=2)
    pe = jnp.stack([cmp_k_pe, cmp_v_pe]).reshape(2, 1, CMP_BLOCK * HEAD_DIM)
    kv_cmp = _compress(tok, tok_next, pe, jnp.stack([cmp_k_w1, cmp_v_w1]), jnp.stack([cmp_k_w2, cmp_v_w2]))
    o_cmp, sel = _cmp_attention(qkv, kv_cmp, _overlap_matrix(nc), min(SEL_TOPK, n_sel))
    o_slc = _slc_attention(qkv, sel)
    o_win = _win_attention(qkv)

    mu = _split_rwkv_cols(rwkv_mu.reshape(1, -1))
    r, lw, k, v, kk, a, g = _rwkv_prep(
        z, mu, rwkv_w0.reshape(1, w), _pad_rows(rwkv_w_up, LORA_PAD), rwkv_a0.reshape(1, w),
        _pad_rows(rwkv_a_up, LORA_PAD), rwkv_g_up, rwkv_k_k.reshape(1, w), rwkv_k_a.reshape(1, w))
    o_b = _rwkv_scan(r, lw, k, v, kk, a, g, rwkv_r_k.reshape(1, w), rwkv_ln_w.reshape(1, w),
                     rwkv_ln_b.reshape(1, w))

    mixed = _merge(o_cmp, o_slc, o_win, bgate, o_b, mgate, proj_nsa.astype(BF16), proj_rwkv.astype(BF16))
    x1, h2 = _outproj(x, mixed, w_out.astype(BF16), norm2_g)
    act = _ffn_up(h2, ffn_up.astype(BF16), ffn_conv_w, ffn_conv_b.reshape(1, -1))
    return _ffn_down(act, ffn_down.astype(BF16), x1, final_g)


def kernel(x, norm1_g, w_in, cmp_k_pe, cmp_k_w1, cmp_k_w2, cmp_v_pe, cmp_v_w1, cmp_v_w2, rwkv_mu, rwkv_w0,
           rwkv_w_up, rwkv_a0, rwkv_a_up, rwkv_g_up, rwkv_k_k, rwkv_k_a, rwkv_r_k, rwkv_ln_w, rwkv_ln_b,
           proj_nsa, proj_rwkv, w_out, norm2_g, ffn_up, ffn_conv_w, ffn_conv_b, ffn_down, final_g):
    assert norm1_g.shape[0] == 1, "single-layer block: the final rmsnorm is fused into the down projection"
    layer = (norm1_g, w_in, cmp_k_pe, cmp_k_w1, cmp_k_w2, cmp_v_pe, cmp_v_w1, cmp_v_w2, rwkv_mu, rwkv_w0,
             rwkv_w_up, rwkv_a0, rwkv_a_up, rwkv_g_up, rwkv_k_k, rwkv_k_a, rwkv_r_k, rwkv_ln_w, rwkv_ln_b,
             proj_nsa, proj_rwkv, w_out, norm2_g, ffn_up, ffn_conv_w, ffn_conv_b, ffn_down)
    weights = [t[0] for t in layer]
    return jnp.stack([_layer(x[b], *weights, final_g) for b in range(x.shape[0])])
```

```python
import functools

import jax
import jax.numpy as jnp
from jax import lax
from jax.experimental import pallas as pl
from jax.experimental.pallas import tpu as pltpu

F32 = jnp.float32
BF16 = jnp.bfloat16
HI = lax.Precision.HIGHEST

LANES = 128
D_MODEL = 2048
HEAD_DIM = 128
NSA_WIDTH = D_MODEL // 2
NSA_HEADS = NSA_WIDTH // HEAD_DIM
NSA_GROUPS = 2
HPG = NSA_HEADS // NSA_GROUPS
KV_WIDTH = NSA_GROUPS * HEAD_DIM
NSA_BRANCHES = 3
CMP_BLOCK = 32
CMP_STRIDE = 16
SEL_BLOCK = 64
SEL_TOPK = 16
N_LOCAL_SEL = 2
WINDOW = 512
Q_BLOCK = 128
ROPE_THETA = 10000.0
ATTN_SCALE = HEAD_DIM ** -0.5
QUERY_SCALE = ATTN_SCALE * 1.4426950408889634
FORCE_BONUS = 1.0e4
NEG = -1.0e30
PICKED = -3.0e38

RWKV_WIDTH = D_MODEL // 2
RWKV_HEAD = 64
DECAY_LORA = 96
ICLR_LORA = 96
GATE_LORA = 256
LORA_PAD = 128
RWKV_Z = 3 * RWKV_WIDTH + 2 * LORA_PAD + GATE_LORA
GN_EPS = 64e-5
SCAN_CHUNK = 64

D_FF = 5632
CONV_WIDTH = 3
EPS = 1e-6

QKV_WIDTH = NSA_WIDTH + 6 * KV_WIDTH
QKV_BLOCKS = QKV_WIDTH // LANES
ROPE_BLOCKS = tuple(range(0, 10)) + (12, 13, 16, 17)
KS_BLOCK, VS_BLOCK, KW_BLOCK, VW_BLOCK = 12, 14, 16, 18

VMEM_LIMIT = 56 * 1024 * 1024


def _params(n_axes):
    return pltpu.CompilerParams(dimension_semantics=("arbitrary",) * n_axes,
                                vmem_limit_bytes=VMEM_LIMIT)


def _dot(a, b, precision=None):
    return jnp.dot(a, b, preferred_element_type=F32, precision=precision)


def _dot_nt(a, b, precision=None):
    return lax.dot_general(a, b, (((1,), (1,)), ((), ())), preferred_element_type=F32,
                           precision=precision)


def _dot_tn(a, b, precision=None):
    return lax.dot_general(a, b, (((0,), (0,)), ((), ())), preferred_element_type=F32,
                           precision=precision)


def _split_bf16(x, parts):
    terms = []
    for _ in range(parts - 1):
        t = x.astype(BF16)
        terms.append(t)
        x = x - t.astype(F32)
    return terms + [x.astype(BF16)]


def _dot_by_01(x, m01):
    m = m01.astype(BF16)
    return functools.reduce(jnp.add, [_dot(t, m) for t in _split_bf16(x, 3)])


def _dot_01_by(m01, x):
    m = m01.astype(BF16)
    return functools.reduce(jnp.add, [_dot(m, t) for t in _split_bf16(x, 3)])


def _dot_split(x, w):
    x_hi, x_lo = _split_bf16(x, 2)
    w_hi, w_lo = _split_bf16(w, 2)
    return _dot(x_hi, w_hi) + (_dot(x_hi, w_lo) + _dot(x_lo, w_hi))


def _iota(shape, dim):
    return lax.broadcasted_iota(jnp.int32, shape, dim)


def _rmsnorm_kernel(x_ref, g_ref, o_ref):
    x = x_ref[...]
    y = x * lax.rsqrt(jnp.mean(x * x, axis=-1, keepdims=True) + EPS)
    o_ref[...] = (y * g_ref[...]).astype(o_ref.dtype)


def _rmsnorm(x, g, out_dtype, tm=512):
    s, d = x.shape
    return pl.pallas_call(
        _rmsnorm_kernel, grid=(s // tm,),
        in_specs=[pl.BlockSpec((tm, d), lambda i: (i, 0)), pl.BlockSpec((1, d), lambda i: (0, 0))],
        out_specs=pl.BlockSpec((tm, d), lambda i: (i, 0)),
        out_shape=jax.ShapeDtypeStruct((s, d), out_dtype),
        compiler_params=_params(1), name="rmsnorm")(x, g.reshape(1, d))


def _mm_kernel(a_ref, b_ref, o_ref):
    o_ref[...] = _dot(a_ref[...], b_ref[...]).astype(o_ref.dtype)


def _matmul(a, b, out_dtype, tm, tn, name):
    m, k = a.shape
    n = b.shape[1]
    return pl.pallas_call(
        _mm_kernel, grid=(m // tm, n // tn),
        in_specs=[pl.BlockSpec((tm, k), lambda i, j: (i, 0)), pl.BlockSpec((k, tn), lambda i, j: (0, j))],
        out_specs=pl.BlockSpec((tm, tn), lambda i, j: (i, j)),
        out_shape=jax.ShapeDtypeStruct((m, n), out_dtype),
        compiler_params=_params(2), name=name)(a, b)


def _rope_kernel(p_ref, cos_ref, sin_ref, o_ref):
    c = cos_ref[...]
    s = sin_ref[...]
    for b in range(QKV_BLOCKS):
        t = p_ref[:, b * LANES:(b + 1) * LANES]
        if b in ROPE_BLOCKS:
            t = t * c + pltpu.roll(t, HEAD_DIM // 2, 1) * s
        if b < NSA_HEADS:
            t = t * QUERY_SCALE
        o_ref[:, b * LANES:(b + 1) * LANES] = t.astype(o_ref.dtype)


def _rope(qkv, cosf, sinf, tm=256):
    s = qkv.shape[0]
    return pl.pallas_call(
        _rope_kernel, grid=(s // tm,),
        in_specs=[pl.BlockSpec((tm, QKV_WIDTH), lambda i: (i, 0)),
                  pl.BlockSpec((tm, HEAD_DIM), lambda i: (i, 0)),
                  pl.BlockSpec((tm, HEAD_DIM), lambda i: (i, 0))],
        out_specs=pl.BlockSpec((tm, QKV_WIDTH), lambda i: (i, 0)),
        out_shape=jax.ShapeDtypeStruct((s, QKV_WIDTH), BF16),
        compiler_params=_params(1), name="rope")(qkv, cosf, sinf)


def _compress_kernel(ta_ref, tb_ref, pe_ref, w1_ref, w2_ref, o_ref):
    half = CMP_STRIDE * HEAD_DIM
    w1 = w1_ref[0]
    w1b = w1.astype(BF16)
    pre = _dot(ta_ref[0, 0], w1b[:half]) + _dot(tb_ref[0, 0], w1b[half:])
    pe = jnp.broadcast_to(pe_ref[0], (8, CMP_BLOCK * HEAD_DIM))
    pre = pre + _dot(pe, w1, HI)[0:1]
    h = jax.nn.gelu(pre)
    o_ref[0, 0] = _dot(h.astype(BF16), w2_ref[0].astype(BF16))


def _compress(ta, tb, pe, w1, w2):
    _, g, nc, width = ta.shape
    tok = pl.BlockSpec((1, 1, nc, width), lambda k, gi: (k, gi, 0, 0))
    return pl.pallas_call(
        _compress_kernel, grid=(2, g),
        in_specs=[tok, tok,
                  pl.BlockSpec((1, 1, CMP_BLOCK * HEAD_DIM), lambda k, gi: (k, 0, 0)),
                  pl.BlockSpec((1, CMP_BLOCK * HEAD_DIM, HEAD_DIM), lambda k, gi: (k, 0, 0)),
                  pl.BlockSpec((1, HEAD_DIM, HEAD_DIM), lambda k, gi: (k, 0, 0))],
        out_specs=pl.BlockSpec((1, 1, nc, HEAD_DIM), lambda k, gi: (k, gi, 0, 0)),
        out_shape=jax.ShapeDtypeStruct((2, g, nc, HEAD_DIM), F32),
        compiler_params=_params(2), name="compress")(ta, tb, pe, w1, w2)


def _stack_heads(q_ref, g=0):
    c0 = g * HPG * HEAD_DIM
    return jnp.concatenate([q_ref[:, c0 + j * HEAD_DIM:c0 + (j + 1) * HEAD_DIM] for j in range(HPG)], axis=0)


def _store_heads(o_ref, o, g=0):
    c0 = g * HPG * HEAD_DIM
    for j in range(HPG):
        o_ref[:, c0 + j * HEAD_DIM:c0 + (j + 1) * HEAD_DIM] = o[j * Q_BLOCK:(j + 1) * Q_BLOCK].astype(o_ref.dtype)


def _cmp_kernel(q_ref, kvc_ref, ov_ref, o_ref, sel_ref, *, nc, top):
    i = pl.program_id(0)
    groups = range(NSA_GROUPS)
    t = i * Q_BLOCK + _iota((Q_BLOCK, 1), 0)
    c_end = _iota((1, nc), 1) * CMP_STRIDE + (CMP_BLOCK - 1)
    cmask = (c_end <= t)[None]
    imp = []
    for g in groups:
        kc = kvc_ref[0, g].astype(BF16)
        vc = kvc_ref[1, g].astype(BF16)
        s = _dot_nt(_stack_heads(q_ref, g), kc).reshape(HPG, Q_BLOCK, nc)
        s = jnp.where(cmask, s, NEG)
        e = jnp.exp2(s - jnp.max(s, axis=-1, keepdims=True))
        p = e / jnp.sum(e, axis=-1, keepdims=True)
        p = jnp.where(cmask, p, 0.0)
        _store_heads(o_ref, _dot(p.reshape(HPG * Q_BLOCK, nc).astype(BF16), vc), g)
        imp.append(_dot_by_01(jnp.sum(p, axis=0), ov_ref[...]))

    jb = _iota((1, LANES), 1)
    cur = jnp.right_shift(t, 6)
    forced = (jb == 0) | ((jb <= cur) & (jb > cur - N_LOCAL_SEL))
    visible = jb * SEL_BLOCK <= t
    val = [jnp.where(visible, x + FORCE_BONUS * forced.astype(F32), NEG) for x in imp]
    sel = [jnp.zeros((Q_BLOCK, LANES), F32) for _ in groups]
    for _ in range(top):
        pick = [jb == jnp.argmax(x, axis=-1, keepdims=True).astype(jnp.int32) for x in val]
        sel = [jnp.where(pick[g], 1.0, sel[g]) for g in groups]
        val = [jnp.where(pick[g], PICKED, val[g]) for g in groups]
    for g in groups:
        sel_ref[:, g * LANES:(g + 1) * LANES] = jnp.where(visible, sel[g], 0.0)


def _cmp_attention(qkv, kvc, overlap, top):
    s = qkv.shape[0]
    nc = kvc.shape[2]
    return pl.pallas_call(
        functools.partial(_cmp_kernel, nc=nc, top=top), grid=(s // Q_BLOCK,),
        in_specs=[pl.BlockSpec((Q_BLOCK, NSA_WIDTH), lambda i: (i, 0)),
                  pl.BlockSpec(kvc.shape, lambda i: (0, 0, 0, 0)),
                  pl.BlockSpec((nc, LANES), lambda i: (0, 0))],
        out_specs=[pl.BlockSpec((Q_BLOCK, NSA_WIDTH), lambda i: (i, 0)),
                   pl.BlockSpec((Q_BLOCK, NSA_GROUPS * LANES), lambda i: (i, 0))],
        out_shape=[jax.ShapeDtypeStruct((s, NSA_WIDTH), F32),
                   jax.ShapeDtypeStruct((s, NSA_GROUPS * LANES), F32)],
        compiler_params=_params(1), name="cmp_attn")(qkv, kvc, overlap)


def _slc_kernel(q_ref, k_ref, v_ref, sel_ref, o_ref, m_ref, l_ref, acc_ref, *, tk):
    i = pl.program_id(1)
    rows = HPG * Q_BLOCK
    q = _stack_heads(q_ref)
    sel = sel_ref[...].astype(BF16)
    qpos = i * Q_BLOCK + _iota((Q_BLOCK, 1), 0)
    blk_off = _iota((LANES, tk), 0) - jnp.right_shift(_iota((LANES, tk), 1), 6)
    blocks_per_tile = tk // SEL_BLOCK
    m_ref[...] = jnp.full(m_ref.shape, NEG, F32)
    l_ref[...] = jnp.zeros(l_ref.shape, F32)
    acc_ref[...] = jnp.zeros(acc_ref.shape, F32)
    last = lax.div(i * Q_BLOCK + Q_BLOCK - 1, tk)

    def tile(kt, causal):
        start = pl.multiple_of(kt * tk, tk)
        expand = jnp.where(blk_off == kt * blocks_per_tile, 1.0, 0.0).astype(BF16)
        chosen = _dot(sel, expand)
        if causal:
            chosen = jnp.where(start + _iota((1, tk), 1) <= qpos, chosen, 0.0)
        s = _dot_nt(q, k_ref[pl.ds(start, tk), :]).reshape(HPG, Q_BLOCK, tk)
        s = jnp.where((chosen > 0.5)[None], s, NEG)
        cols = [s[:, :, c * LANES:(c + 1) * LANES] for c in range(tk // LANES)]
        m_old = m_ref[...]
        m_new = jnp.maximum(m_old, jnp.max(functools.reduce(jnp.maximum, cols), axis=-1, keepdims=True))
        alpha = jnp.exp2(m_old - m_new)
        ps = [jnp.exp2(col - m_new) for col in cols]
        m_ref[...] = m_new
        l_ref[...] = alpha * l_ref[...] + functools.reduce(jnp.add, ps)
        p = jnp.concatenate(ps, axis=-1).reshape(rows, tk).astype(BF16)
        acc_ref[...] = alpha.reshape(rows, LANES) * acc_ref[...] + _dot(p, v_ref[pl.ds(start, tk), :])

    def body(j, carry):
        tile(2 * j, False)
        tile(2 * j + 1, False)
        return carry

    lax.fori_loop(0, lax.div(last, 2), body, 0)

    @pl.when(lax.rem(last, 2) == 1)
    def _():
        tile(last - 1, False)

    tile(last, True)
    m = m_ref[...].reshape(rows, LANES)
    l = jnp.sum(l_ref[...], axis=-1, keepdims=True).reshape(rows, 1)
    _store_heads(o_ref, jnp.where(m > 0.5 * NEG, acc_ref[...] / l, 0.0))


def _slc_attention(qkv, sel, tk=512):
    s = qkv.shape[0]
    gw = HPG * HEAD_DIM
    return pl.pallas_call(
        functools.partial(_slc_kernel, tk=tk), grid=(NSA_GROUPS, s // Q_BLOCK),
        in_specs=[pl.BlockSpec((Q_BLOCK, gw), lambda g, i: (i, g)),
                  pl.BlockSpec((s, HEAD_DIM), lambda g, i: (0, KS_BLOCK + g)),
                  pl.BlockSpec((s, HEAD_DIM), lambda g, i: (0, VS_BLOCK + g)),
                  pl.BlockSpec((Q_BLOCK, LANES), lambda g, i: (i, g))],
        out_specs=pl.BlockSpec((Q_BLOCK, gw), lambda g, i: (i, g)),
        out_shape=jax.ShapeDtypeStruct((s, NSA_WIDTH), F32),
        scratch_shapes=[pltpu.VMEM((HPG, Q_BLOCK, LANES), F32), pltpu.VMEM((HPG, Q_BLOCK, LANES), F32),
                        pltpu.VMEM((HPG * Q_BLOCK, HEAD_DIM), F32)],
        compiler_params=_params(2), name="slc_attn")(qkv, qkv, qkv, sel)


def _win_kernel(q_ref, k_ref, v_ref, o_ref):
    i = pl.program_id(1)
    band = WINDOW + Q_BLOCK
    start = pl.multiple_of(jnp.maximum(i * Q_BLOCK - WINDOW, 0), Q_BLOCK)
    k_b = k_ref[pl.ds(start, band), :]
    v_b = v_ref[pl.ds(start, band), :]
    qpos = i * Q_BLOCK + _iota((Q_BLOCK, 1), 0)
    kpos = start + _iota((1, band), 1)
    allowed = ((kpos <= qpos) & (kpos > qpos - WINDOW))[None]
    s = _dot_nt(_stack_heads(q_ref), k_b).reshape(HPG, Q_BLOCK, band)
    s = jnp.where(allowed, s, NEG)
    e = jnp.exp2(s - jnp.max(s, axis=-1, keepdims=True))
    p = e / jnp.sum(e, axis=-1, keepdims=True)
    _store_heads(o_ref, _dot(p.reshape(HPG * Q_BLOCK, band).astype(BF16), v_b))


def _win_attention(qkv):
    s = qkv.shape[0]
    gw = HPG * HEAD_DIM
    return pl.pallas_call(
        _win_kernel, grid=(NSA_GROUPS, s // Q_BLOCK),
        in_specs=[pl.BlockSpec((Q_BLOCK, gw), lambda g, i: (i, g)),
                  pl.BlockSpec((s, HEAD_DIM), lambda g, i: (0, KW_BLOCK + g)),
                  pl.BlockSpec((s, HEAD_DIM), lambda g, i: (0, VW_BLOCK + g))],
        out_specs=pl.BlockSpec((Q_BLOCK, gw), lambda g, i: (i, g)),
        out_shape=jax.ShapeDtypeStruct((s, NSA_WIDTH), F32),
        compiler_params=_params(2), name="win_attn")(qkv, qkv, qkv)


def _head_sum(x):
    seg = (jnp.right_shift(_iota((LANES, LANES), 0), 6) ==
           jnp.right_shift(_iota((LANES, LANES), 1), 6)).astype(F32)
    return _dot_by_01(x, seg)


def _rwkv_prep_kernel(z_ref, zp_ref, mu_ref, w0_ref, wup_ref, a0_ref, aup_ref, gup_ref, kscale_ref, ka_ref,
                      r_out, lw_out, k_out, v_out, kk_out, a_out, g_out):
    i = pl.program_id(0)
    z = z_ref[...]
    tm = z.shape[0]
    last = jnp.where(i == 0, 0.0, zp_ref[7:8, :])
    z_prev = jnp.where(_iota((tm, 1), 0) == 0, last, pltpu.roll(z, 1, 0))
    z = z + (z_prev - z) * mu_ref[...]
    w = RWKV_WIDTH
    r, k, v = z[:, 0:w], z[:, w:2 * w], z[:, 2 * w:3 * w]
    wd = z[:, 3 * w:3 * w + LORA_PAD]
    ad = z[:, 3 * w + LORA_PAD:3 * w + 2 * LORA_PAD]
    gd = z[:, 3 * w + 2 * LORA_PAD:]
    wlog = -jax.nn.softplus(-(w0_ref[...] + _dot_split(jnp.tanh(wd), wup_ref[...]))) - 0.5
    a = jax.nn.sigmoid(a0_ref[...] + _dot(ad.astype(BF16), aup_ref[...].astype(BF16)))
    g = _dot(jax.nn.sigmoid(gd).astype(BF16), gup_ref[...].astype(BF16))
    kk = k * kscale_ref[...]
    for b in range(w // LANES):
        sl = slice(b * LANES, (b + 1) * LANES)
        kb = kk[:, sl]
        kk_out[:, sl] = kb * lax.rsqrt(jnp.maximum(_head_sum(kb * kb), 1e-24))
    r_out[...] = r
    lw_out[...] = -jnp.exp(wlog)
    k_out[...] = k * (1.0 + (a - 1.0) * ka_ref[...])
    v_out[...] = v
    a_out[...] = a
    g_out[...] = g


def _rwkv_prep(z, mu, w0, w_up, a0, a_up, g_up, k_k, k_a, tm=256):
    s = z.shape[0]
    w = RWKV_WIDTH
    row = lambda n: pl.BlockSpec((1, n), lambda i: (0, 0))
    full = lambda a: pl.BlockSpec(a.shape, lambda i: (0, 0))
    out = pl.BlockSpec((tm, w), lambda i: (i, 0))
    return pl.pallas_call(
        _rwkv_prep_kernel, grid=(s // tm,),
        in_specs=[pl.BlockSpec((tm, RWKV_Z), lambda i: (i, 0)),
                  pl.BlockSpec((8, RWKV_Z), lambda i: (jnp.maximum(i * (tm // 8) - 1, 0), 0)),
                  row(RWKV_Z), row(w), full(w_up), row(w), full(a_up), full(g_up), row(w), row(w)],
        out_specs=[out] * 7,
        out_shape=[jax.ShapeDtypeStruct((s, w), F32)] * 7,
        compiler_params=_params(1), name="rwkv_prep")(z, z, mu, w0, w_up, a0, a_up, g_up, k_k, k_a)


def _scan_kernel(r_ref, lw_ref, k_ref, v_ref, kk_ref, a_ref, g_ref, rk_ref, lnw_ref, lnb_ref,
                 o_ref, state_ref):
    c = SCAN_CHUNK

    @pl.when(pl.program_id(0) == 0)
    def _():
        state_ref[...] = jnp.zeros_like(state_ref)

    tri = (_iota((c, c), 0) >= _iota((c, c), 1)).astype(F32)
    head0 = _iota((1, LANES), 1) < RWKV_HEAD
    tok_r = jnp.bitwise_and(_iota((LANES, LANES), 0), c - 1)
    tok_c = jnp.bitwise_and(_iota((LANES, LANES), 1), c - 1)
    strict = tok_r > tok_c
    incl = tok_r >= tok_c
    eye = (_iota((LANES, LANES), 0) == _iota((LANES, LANES), 1)).astype(F32)

    def stack(x):
        return jnp.concatenate([jnp.where(head0, x, 0.0), jnp.where(head0, 0.0, x)], axis=0)

    pairs = range(RWKV_WIDTH // LANES)
    sls = [slice(p * LANES, (p + 1) * LANES) for p in pairs]
    lw = [lw_ref[:, sl] for sl in sls]
    cum = [_dot_01_by(tri, x) for x in lw]
    total = [x[c - 1:c, :] for x in cum]
    r = [r_ref[:, sl] for sl in sls]
    k = [k_ref[:, sl] for sl in sls]
    v = [v_ref[:, sl] for sl in sls]
    kk = [kk_ref[:, sl] for sl in sls]
    b = [kk[p] * a_ref[:, sls[p]] for p in pairs]
    e_neg = [jnp.exp(-x) for x in cum]
    a2b = [stack(-kk[p] * jnp.exp(cum[p] - lw[p])).astype(BF16) for p in pairs]
    r2b = [stack(r[p] * jnp.exp(cum[p])).astype(BF16) for p in pairs]
    b2 = [stack(b[p] * e_neg[p]) for p in pairs]
    k2 = [stack(k[p] * e_neg[p]) for p in pairs]
    v2 = [stack(x).astype(BF16) for x in v]
    gram = [_dot_nt(jnp.concatenate([a2b[p], r2b[p]], axis=0),
                    jnp.concatenate([b2[p], k2[p]], axis=0).astype(BF16)) for p in pairs]
    l_ab = [jnp.where(strict, x[0:LANES, 0:LANES], 0.0) for x in gram]
    m_ak = [jnp.where(strict, x[0:LANES, LANES:], 0.0).astype(BF16) for x in gram]
    m_rb = [jnp.where(incl, x[LANES:, 0:LANES], 0.0).astype(BF16) for x in gram]
    m_rk = [jnp.where(incl, x[LANES:, LANES:], 0.0).astype(BF16) for x in gram]
    inv = [eye + x for x in l_ab]
    pw = l_ab
    for _ in range(5):
        pwb = [x.astype(BF16) for x in pw]
        pw = [_dot(x, x) for x in pwb]
        inv = [inv[p] + _dot(inv[p].astype(BF16), pw[p].astype(BF16)) for p in pairs]
    state = [state_ref[p] for p in pairs]
    sb = [x.astype(BF16) for x in state]
    x = [_dot_nt(a2b[p], sb[p]) + _dot(m_ak[p], v2[p]) for p in pairs]
    u2 = [_dot(inv[p].astype(BF16), x[p].astype(BF16)).astype(BF16) for p in pairs]
    y2 = [_dot_nt(r2b[p], sb[p]) + _dot(m_rb[p], u2[p]) + _dot(m_rk[p], v2[p]) for p in pairs]
    for p in pairs:
        e_end = jnp.exp(total[p] - cum[p])
        uv = jnp.concatenate([u2[p], v2[p]], axis=0)
        bk = jnp.concatenate([stack(b[p] * e_end), stack(k[p] * e_end)], axis=0).astype(BF16)
        state_ref[p] = state[p] * jnp.exp(total[p]) + _dot_tn(uv, bk)
    y = [x[0:c] + x[c:] for x in y2]
    mean = [_head_sum(x) * (1.0 / RWKV_HEAD) for x in y]
    d = [y[p] - mean[p] for p in pairs]
    var = [_head_sum(x * x) * (1.0 / RWKV_HEAD) for x in d]
    bonus = [_head_sum(r[p] * k[p] * rk_ref[:, sls[p]]) for p in pairs]
    for p in pairs:
        sl = sls[p]
        out = d[p] * lax.rsqrt(var[p] + GN_EPS) * lnw_ref[:, sl] + lnb_ref[:, sl]
        o_ref[:, sl] = (out + bonus[p] * v[p]) * g_ref[:, sl]


def _rwkv_scan(r, lw, k, v, kk, a, g, r_k, ln_w, ln_b):
    s, w = r.shape
    blk = pl.BlockSpec((SCAN_CHUNK, w), lambda c: (c, 0))
    row = pl.BlockSpec((1, w), lambda c: (0, 0))
    return pl.pallas_call(
        _scan_kernel, grid=(s // SCAN_CHUNK,),
        in_specs=[blk] * 7 + [row] * 3,
        out_specs=blk,
        out_shape=jax.ShapeDtypeStruct((s, w), F32),
        scratch_shapes=[pltpu.VMEM((w // LANES, LANES, LANES), F32)],
        compiler_params=_params(1), name="rwkv_scan")(r, lw, k, v, kk, a, g, r_k, ln_w, ln_b)


def _merge_kernel(oc_ref, os_ref, ow_ref, bg_ref, ob_ref, mg_ref, pa_ref, pb_ref, o_ref):
    bg = jax.nn.sigmoid(bg_ref[...])
    heads = []
    for h in range(NSA_HEADS):
        sl = slice(h * HEAD_DIM, (h + 1) * HEAD_DIM)
        c0 = NSA_BRANCHES * h
        heads.append(bg[:, c0:c0 + 1] * oc_ref[:, sl] + bg[:, c0 + 1:c0 + 2] * os_ref[:, sl]
                     + bg[:, c0 + 2:c0 + 3] * ow_ref[:, sl])
    o_a = jnp.concatenate(heads, axis=-1).astype(BF16)
    y_a = _dot(o_a, pa_ref[...])
    y_b = _dot(ob_ref[...].astype(BF16), pb_ref[...])
    mg = jax.nn.sigmoid(mg_ref[...].astype(F32))
    o_ref[...] = (mg[:, :D_MODEL] * y_a + mg[:, D_MODEL:] * y_b).astype(o_ref.dtype)


def _merge(o_cmp, o_slc, o_win, bgate, o_b, mgate, proj_a, proj_b, tm=256):
    s = o_cmp.shape[0]
    half = lambda: pl.BlockSpec((tm, NSA_WIDTH), lambda i: (i, 0))
    wspec = pl.BlockSpec((NSA_WIDTH, D_MODEL), lambda i: (0, 0))
    return pl.pallas_call(
        _merge_kernel, grid=(s // tm,),
        in_specs=[half(), half(), half(), pl.BlockSpec((tm, LANES), lambda i: (i, 0)), half(),
                  pl.BlockSpec((tm, 2 * D_MODEL), lambda i: (i, 0)), wspec, wspec],
        out_specs=pl.BlockSpec((tm, D_MODEL), lambda i: (i, 0)),
        out_shape=jax.ShapeDtypeStruct((s, D_MODEL), BF16),
        compiler_params=_params(1), name="merge")(o_cmp, o_slc, o_win, bgate, o_b, mgate, proj_a, proj_b)


def _outproj_kernel(x_ref, m_ref, w_ref, g_ref, x1_ref, h_ref):
    x1 = x_ref[...] + _dot(m_ref[...], w_ref[...])
    x1_ref[...] = x1
    y = x1 * lax.rsqrt(jnp.mean(x1 * x1, axis=-1, keepdims=True) + EPS)
    h_ref[...] = (y * g_ref[...]).astype(h_ref.dtype)


def _outproj(x, mixed, w_out, g, tm=256):
    s, d = x.shape
    blk = pl.BlockSpec((tm, d), lambda i: (i, 0))
    return pl.pallas_call(
        _outproj_kernel, grid=(s // tm,),
        in_specs=[blk, blk, pl.BlockSpec((d, d), lambda i: (0, 0)), pl.BlockSpec((1, d), lambda i: (0, 0))],
        out_specs=[blk, blk],
        out_shape=[jax.ShapeDtypeStruct((s, d), F32), jax.ShapeDtypeStruct((s, d), BF16)],
        compiler_params=_params(1), name="outproj")(x, mixed, w_out, g.reshape(1, d))


def _ffn_up_kernel(h_ref, wg_ref, wv_ref, cwg_ref, cwv_ref, cbg_ref, cbv_ref, o_ref, ug_ref, uv_ref):
    tm = h_ref.shape[0]

    @pl.when(pl.program_id(1) == 0)
    def _():
        ug_ref[0:8, :] = jnp.zeros((8, ug_ref.shape[1]), F32)
        uv_ref[0:8, :] = jnp.zeros((8, uv_ref.shape[1]), F32)

    h = h_ref[...]

    def conv(u_ref, w_ref, cw_ref, cb_ref):
        u_ref[8:8 + tm, :] = _dot(h, w_ref[...])
        out = cb_ref[...]
        for tap in range(CONV_WIDTH):
            off = 8 - (CONV_WIDTH - 1 - tap)
            out = out + u_ref[off:off + tm, :] * cw_ref[tap:tap + 1, :]
        u_ref[0:8, :] = u_ref[tm:tm + 8, :]
        return out

    gate = conv(ug_ref, wg_ref, cwg_ref, cbg_ref)
    val = conv(uv_ref, wv_ref, cwv_ref, cbv_ref)
    o_ref[...] = (jax.nn.silu(gate) * val).astype(o_ref.dtype)


def _ffn_up(h, w_up, conv_w, conv_b, tm=512, tn=512):
    s, d = h.shape
    nj = D_FF // tn
    return pl.pallas_call(
        _ffn_up_kernel, grid=(nj, s // tm),
        in_specs=[pl.BlockSpec((tm, d), lambda j, i: (i, 0)),
                  pl.BlockSpec((d, tn), lambda j, i: (0, j)),
                  pl.BlockSpec((d, tn), lambda j, i: (0, nj + j)),
                  pl.BlockSpec((CONV_WIDTH, tn), lambda j, i: (0, j)),
                  pl.BlockSpec((CONV_WIDTH, tn), lambda j, i: (0, nj + j)),
                  pl.BlockSpec((1, tn), lambda j, i: (0, j)),
                  pl.BlockSpec((1, tn), lambda j, i: (0, nj + j))],
        out_specs=pl.BlockSpec((tm, tn), lambda j, i: (i, j)),
        out_shape=jax.ShapeDtypeStruct((s, D_FF), BF16),
        scratch_shapes=[pltpu.VMEM((tm + 8, tn), F32), pltpu.VMEM((tm + 8, tn), F32)],
        compiler_params=_params(2), name="ffn_up")(h, w_up, w_up, conv_w, conv_w, conv_b, conv_b)


def _ffn_down_kernel(a_ref, w_ref, x_ref, g_ref, o_ref, acc_ref):
    kk = pl.program_id(1)

    @pl.when(kk == 0)
    def _():
        acc_ref[...] = x_ref[...]

    acc_ref[...] += _dot(a_ref[...], w_ref[...])

    @pl.when(kk == pl.num_programs(1) - 1)
    def _():
        x2 = acc_ref[...]
        y = x2 * lax.rsqrt(jnp.mean(x2 * x2, axis=-1, keepdims=True) + EPS)
        o_ref[...] = y * g_ref[...]


def _ffn_down(act, w_down, x1, g, tm=512, tk=512):
    s, d = x1.shape
    return pl.pallas_call(
        _ffn_down_kernel, grid=(s // tm, D_FF // tk),
        in_specs=[pl.BlockSpec((tm, tk), lambda i, k: (i, k)),
                  pl.BlockSpec((tk, d), lambda i, k: (k, 0)),
                  pl.BlockSpec((tm, d), lambda i, k: (i, 0)),
                  pl.BlockSpec((1, d), lambda i, k: (0, 0))],
        out_specs=pl.BlockSpec((tm, d), lambda i, k: (i, 0)),
        out_shape=jax.ShapeDtypeStruct((s, d), F32),
        scratch_shapes=[pltpu.VMEM((tm, d), F32)],
        compiler_params=_params(2), name="ffn_down")(act, w_down, x1, g.reshape(1, d))


def _pad_cols(w, n):
    return jnp.pad(w, ((0, 0), (0, n - w.shape[1])))


def _pad_rows(w, n):
    return jnp.pad(w, ((0, n - w.shape[0]), (0, 0)))


def _split_rwkv_cols(t):
    w = RWKV_WIDTH
    rkv, wd, ad, gd = (t[:, :3 * w], t[:, 3 * w:3 * w + DECAY_LORA],
                       t[:, 3 * w + DECAY_LORA:3 * w + DECAY_LORA + ICLR_LORA],
                       t[:, 3 * w + DECAY_LORA + ICLR_LORA:])
    return jnp.concatenate([rkv, _pad_cols(wd, LORA_PAD), _pad_cols(ad, LORA_PAD), gd], axis=1)


def _rope_tables(seq):
    half = HEAD_DIM // 2
    inv = ROPE_THETA ** (-jnp.arange(half, dtype=F32) / half)
    ang = jnp.arange(seq, dtype=F32)[:, None] * inv[None, :]
    cos, sin = jnp.cos(ang), jnp.sin(ang)
    return jnp.concatenate([cos, cos], axis=1), jnp.concatenate([-sin, sin], axis=1)


def _overlap_matrix(nc):
    c_start = jnp.arange(nc) * CMP_STRIDE
    s_start = jnp.arange(LANES) * SEL_BLOCK
    ov = jnp.clip(jnp.minimum(c_start[:, None] + CMP_BLOCK, s_start[None, :] + SEL_BLOCK)
                  - jnp.maximum(c_start[:, None], s_start[None, :]), 0)
    return ov.astype(F32) / CMP_BLOCK


def _layer(x, norm1_g, w_in, cmp_k_pe, cmp_k_w1, cmp_k_w2, cmp_v_pe, cmp_v_w1, cmp_v_w2, rwkv_mu, rwkv_w0,
           rwkv_w_up, rwkv_a0, rwkv_a_up, rwkv_g_up, rwkv_k_k, rwkv_k_a, rwkv_r_k, rwkv_ln_w, rwkv_ln_b,
           proj_nsa, proj_rwkv, w_out, norm2_g, ffn_up, ffn_conv_w, ffn_conv_b, ffn_down, final_g):
    seq = x.shape[0]
    n_sel = seq // SEL_BLOCK
    assert seq % 2048 == 0 and n_sel <= LANES
    nc = seq // CMP_STRIDE
    w = RWKV_WIDTH

    c_gate = QKV_WIDTH
    c_z = c_gate + NSA_HEADS * NSA_BRANCHES
    c_merge = c_z + 3 * w + DECAY_LORA + ICLR_LORA + GATE_LORA
    w_qkv = w_in[:, :c_gate].astype(BF16)
    w_bgate = _pad_cols(w_in[:, c_gate:c_z], LANES).astype(BF16)
    w_z = _split_rwkv_cols(w_in[:, c_z:c_merge]).astype(BF16)
    w_merge = w_in[:, c_merge:].astype(BF16)

    h = _rmsnorm(x, norm1_g, BF16)
    qkv = _matmul(h, w_qkv, F32, 1024, 1280, "proj_qkv")
    bgate = _matmul(h, w_bgate, F32, 1024, LANES, "proj_bgate")
    z = _matmul(h, w_z, F32, 1024, 896, "proj_rwkv")
    mgate = _matmul(h, w_merge, BF16, 1024, 1024, "proj_merge")

    cosf, sinf = _rope_tables(seq)
    qkv = _rope(qkv, cosf, sinf)
    kv_c = qkv[:, NSA_WIDTH:NSA_WIDTH + 2 * KV_WIDTH]
    tok = kv_c.reshape(nc, CMP_STRIDE, 2, NSA_GROUPS, HEAD_DIM)
    tok = jnp.transpose(tok, (2, 3, 0, 1, 4)).reshape(2, NSA_GROUPS, nc, CMP_STRIDE * HEAD_DIM)
    tok_next = jnp.concatenate([tok[:, :, 1:], jnp.zeros_like(tok[:, :, :1])], axis=2)
    pe = jnp.stack([cmp_k_pe, cmp_v_pe]).reshape(2, 1, CMP_BLOCK * HEAD_DIM)
    kv_cmp = _compress(tok, tok_next, pe, jnp.stack([cmp_k_w1, cmp_v_w1]), jnp.stack([cmp_k_w2, cmp_v_w2]))
    o_cmp, sel = _cmp_attention(qkv, kv_cmp, _overlap_matrix(nc), min(SEL_TOPK, n_sel))
    o_slc = _slc_attention(qkv, sel)
    o_win = _win_attention(qkv)

    mu = _split_rwkv_cols(rwkv_mu.reshape(1, -1))
    r, lw, k, v, kk, a, g = _rwkv_prep(
        z, mu, rwkv_w0.reshape(1, w), _pad_rows(rwkv_w_up, LORA_PAD), rwkv_a0.reshape(1, w),
        _pad_rows(rwkv_a_up, LORA_PAD), rwkv_g_up, rwkv_k_k.reshape(1, w), rwkv_k_a.reshape(1, w))
    o_b = _rwkv_scan(r, lw, k, v, kk, a, g, rwkv_r_k.reshape(1, w), rwkv_ln_w.reshape(1, w),
                     rwkv_ln_b.reshape(1, w))

    mixed = _merge(o_cmp, o_slc, o_win, bgate, o_b, mgate, proj_nsa.astype(BF16), proj_rwkv.astype(BF16))
    x1, h2 = _outproj(x, mixed, w_out.astype(BF16), norm2_g)
    act = _ffn_up(h2, ffn_up.astype(BF16), ffn_conv_w, ffn_conv_b.reshape(1, -1))
    return _ffn_down(act, ffn_down.astype(BF16), x1, final_g)


def kernel(x, norm1_g, w_in, cmp_k_pe, cmp_k_w1, cmp_k_w2, cmp_v_pe, cmp_v_w1, cmp_v_w2, rwkv_mu, rwkv_w0,
           rwkv_w_up, rwkv_a0, rwkv_a_up, rwkv_g_up, rwkv_k_k, rwkv_k_a, rwkv_r_k, rwkv_ln_w, rwkv_ln_b,
           proj_nsa, proj_rwkv, w_out, norm2_g, ffn_up, ffn_conv_w, ffn_conv_b, ffn_down, final_g):
    assert norm1_g.shape[0] == 1, "single-layer block: the final rmsnorm is fused into the down projection"
    layer = (norm1_g, w_in, cmp_k_pe, cmp_k_w1, cmp_k_w2, cmp_v_pe, cmp_v_w1, cmp_v_w2, rwkv_mu, rwkv_w0,
             rwkv_w_up, rwkv_a0, rwkv_a_up, rwkv_g_up, rwkv_k_k, rwkv_k_a, rwkv_r_k, rwkv_ln_w, rwkv_ln_b,
             proj_nsa, proj_rwkv, w_out, norm2_g, ffn_up, ffn_conv_w, ffn_conv_b, ffn_down)
    weights = [t[0] for t in layer]
    return jnp.stack([_layer(x[b], *weights, final_g) for b in range(x.shape[0])])
```

```python
import functools

import jax
import jax.numpy as jnp
from jax import lax
from jax.experimental import pallas as pl
from jax.experimental.pallas import tpu as pltpu

F32 = jnp.float32
BF16 = jnp.bfloat16
HI = lax.Precision.HIGHEST

LANES = 128
D_MODEL = 2048
HEAD_DIM = 128
NSA_WIDTH = D_MODEL // 2
NSA_HEADS = NSA_WIDTH // HEAD_DIM
NSA_GROUPS = 2
HPG = NSA_HEADS // NSA_GROUPS
KV_WIDTH = NSA_GROUPS * HEAD_DIM
NSA_BRANCHES = 3
CMP_BLOCK = 32
CMP_STRIDE = 16
SEL_BLOCK = 64
SEL_TOPK = 16
N_LOCAL_SEL = 2
WINDOW = 512
Q_BLOCK = 128
ROPE_THETA = 10000.0
ATTN_SCALE = HEAD_DIM ** -0.5
QUERY_SCALE = ATTN_SCALE * 1.4426950408889634
FORCE_BONUS = 1.0e4
NEG = -1.0e30
PICKED = -3.0e38

RWKV_WIDTH = D_MODEL // 2
RWKV_HEAD = 64
DECAY_LORA = 96
ICLR_LORA = 96
GATE_LORA = 256
LORA_PAD = 128
RWKV_Z = 3 * RWKV_WIDTH + 2 * LORA_PAD + GATE_LORA
GN_EPS = 64e-5
SCAN_CHUNK = 64

D_FF = 5632
CONV_WIDTH = 3
EPS = 1e-6

QKV_WIDTH = NSA_WIDTH + 6 * KV_WIDTH
QKV_BLOCKS = QKV_WIDTH // LANES
ROPE_BLOCKS = tuple(range(0, 10)) + (12, 13, 16, 17)
KS_BLOCK, VS_BLOCK, KW_BLOCK, VW_BLOCK = 12, 14, 16, 18

VMEM_LIMIT = 56 * 1024 * 1024


def _params(n_axes):
    return pltpu.CompilerParams(dimension_semantics=("arbitrary",) * n_axes,
                                vmem_limit_bytes=VMEM_LIMIT)


def _dot(a, b, precision=None):
    return jnp.dot(a, b, preferred_element_type=F32, precision=precision)


def _dot_nt(a, b, precision=None):
    return lax.dot_general(a, b, (((1,), (1,)), ((), ())), preferred_element_type=F32,
                           precision=precision)


def _dot_tn(a, b, precision=None):
    return lax.dot_general(a, b, (((0,), (0,)), ((), ())), preferred_element_type=F32,
                           precision=precision)


def _split_bf16(x, parts):
    terms = []
    for _ in range(parts - 1):
        t = x.astype(BF16)
        terms.append(t)
        x = x - t.astype(F32)
    return terms + [x.astype(BF16)]


def _dot_by_01(x, m01):
    m = m01.astype(BF16)
    return functools.reduce(jnp.add, [_dot(t, m) for t in _split_bf16(x, 3)])


def _dot_01_by(m01, x):
    m = m01.astype(BF16)
    return functools.reduce(jnp.add, [_dot(m, t) for t in _split_bf16(x, 3)])


def _dot_split(x, w):
    x_hi, x_lo = _split_bf16(x, 2)
    w_hi, w_lo = _split_bf16(w, 2)
    return _dot(x_hi, w_hi) + (_dot(x_hi, w_lo) + _dot(x_lo, w_hi))


def _iota(shape, dim):
    return lax.broadcasted_iota(jnp.int32, shape, dim)


def _rmsnorm_kernel(x_ref, g_ref, o_ref):
    x = x_ref[...]
    y = x * lax.rsqrt(jnp.mean(x * x, axis=-1, keepdims=True) + EPS)
    o_ref[...] = (y * g_ref[...]).astype(o_ref.dtype)


def _rmsnorm(x, g, out_dtype, tm=512):
    s, d = x.shape
    return pl.pallas_call(
        _rmsnorm_kernel, grid=(s // tm,),
        in_specs=[pl.BlockSpec((tm, d), lambda i: (i, 0)), pl.BlockSpec((1, d), lambda i: (0, 0))],
        out_specs=pl.BlockSpec((tm, d), lambda i: (i, 0)),
        out_shape=jax.ShapeDtypeStruct((s, d), out_dtype),
        compiler_params=_params(1), name="rmsnorm")(x, g.reshape(1, d))


def _mm_kernel(a_ref, b_ref, o_ref):
    o_ref[...] = _dot(a_ref[...], b_ref[...]).astype(o_ref.dtype)


def _matmul(a, b, out_dtype, tm, tn, name):
    m, k = a.shape
    n = b.shape[1]
    return pl.pallas_call(
        _mm_kernel, grid=(m // tm, n // tn),
        in_specs=[pl.BlockSpec((tm, k), lambda i, j: (i, 0)), pl.BlockSpec((k, tn), lambda i, j: (0, j))],
        out_specs=pl.BlockSpec((tm, tn), lambda i, j: (i, j)),
        out_shape=jax.ShapeDtypeStruct((m, n), out_dtype),
        compiler_params=_params(2), name=name)(a, b)


def _rope_kernel(p_ref, cos_ref, sin_ref, o_ref):
    c = cos_ref[...]
    s = sin_ref[...]
    for b in range(QKV_BLOCKS):
        t = p_ref[:, b * LANES:(b + 1) * LANES]
        if b in ROPE_BLOCKS:
            t = t * c + pltpu.roll(t, HEAD_DIM // 2, 1) * s
        if b < NSA_HEADS:
            t = t * QUERY_SCALE
        o_ref[:, b * LANES:(b + 1) * LANES] = t.astype(o_ref.dtype)


def _rope(qkv, cosf, sinf, tm=256):
    s = qkv.shape[0]
    return pl.pallas_call(
        _rope_kernel, grid=(s // tm,),
        in_specs=[pl.BlockSpec((tm, QKV_WIDTH), lambda i: (i, 0)),
                  pl.BlockSpec((tm, HEAD_DIM), lambda i: (i, 0)),
                  pl.BlockSpec((tm, HEAD_DIM), lambda i: (i, 0))],
        out_specs=pl.BlockSpec((tm, QKV_WIDTH), lambda i: (i, 0)),
        out_shape=jax.ShapeDtypeStruct((s, QKV_WIDTH), BF16),
        compiler_params=_params(1), name="rope")(qkv, cosf, sinf)


def _compress_kernel(ta_ref, tb_ref, pe_ref, w1_ref, w2_ref, o_ref):
    half = CMP_STRIDE * HEAD_DIM
    w1 = w1_ref[0]
    w1b = w1.astype(BF16)
    pre = _dot(ta_ref[0, 0], w1b[:half]) + _dot(tb_ref[0, 0], w1b[half:])
    pe = jnp.broadcast_to(pe_ref[0], (8, CMP_BLOCK * HEAD_DIM))
    pre = pre + _dot(pe, w1, HI)[0:1]
    h = jax.nn.gelu(pre)
    o_ref[0, 0] = _dot(h.astype(BF16), w2_ref[0].astype(BF16))


def _compress(ta, tb, pe, w1, w2):
    _, g, nc, width = ta.shape
    tok = pl.BlockSpec((1, 1, nc, width), lambda k, gi: (k, gi, 0, 0))
    return pl.pallas_call(
        _compress_kernel, grid=(2, g),
        in_specs=[tok, tok,
                  pl.BlockSpec((1, 1, CMP_BLOCK * HEAD_DIM), lambda k, gi: (k, 0, 0)),
                  pl.BlockSpec((1, CMP_BLOCK * HEAD_DIM, HEAD_DIM), lambda k, gi: (k, 0, 0)),
                  pl.BlockSpec((1, HEAD_DIM, HEAD_DIM), lambda k, gi: (k, 0, 0))],
        out_specs=pl.BlockSpec((1, 1, nc, HEAD_DIM), lambda k, gi: (k, gi, 0, 0)),
        out_shape=jax.ShapeDtypeStruct((2, g, nc, HEAD_DIM), F32),
        compiler_params=_params(2), name="compress")(ta, tb, pe, w1, w2)


def _stack_heads(q_ref, g=0):
    c0 = g * HPG * HEAD_DIM
    return jnp.concatenate([q_ref[:, c0 + j * HEAD_DIM:c0 + (j + 1) * HEAD_DIM] for j in range(HPG)], axis=0)


def _store_heads(o_ref, o, g=0):
    c0 = g * HPG * HEAD_DIM
    for j in range(HPG):
        o_ref[:, c0 + j * HEAD_DIM:c0 + (j + 1) * HEAD_DIM] = o[j * Q_BLOCK:(j + 1) * Q_BLOCK].astype(o_ref.dtype)


def _cmp_kernel(q_ref, kvc_ref, ov_ref, o_ref, sel_ref, *, nc, top):
    i = pl.program_id(0)
    groups = range(NSA_GROUPS)
    t = i * Q_BLOCK + _iota((Q_BLOCK, 1), 0)
    c_end = _iota((1, nc), 1) * CMP_STRIDE + (CMP_BLOCK - 1)
    cmask = (c_end <= t)[None]
    imp = []
    for g in groups:
        kc = kvc_ref[0, g].astype(BF16)
        vc = kvc_ref[1, g].astype(BF16)
        s = _dot_nt(_stack_heads(q_ref, g), kc).reshape(HPG, Q_BLOCK, nc)
        s = jnp.where(cmask, s, NEG)
        e = jnp.exp2(s - jnp.max(s, axis=-1, keepdims=True))
        p = e / jnp.sum(e, axis=-1, keepdims=True)
        p = jnp.where(cmask, p, 0.0)
        _store_heads(o_ref, _dot(p.reshape(HPG * Q_BLOCK, nc).astype(BF16), vc), g)
        imp.append(functools.reduce(jnp.add, [_dot_nt(ov_ref[...].astype(BF16), term)
                                              for term in _split_bf16(jnp.sum(p, axis=0), 3)]))

    jb = _iota((LANES, 1), 0)
    tq = i * Q_BLOCK + _iota((1, Q_BLOCK), 1)
    cur = jnp.right_shift(tq, 6)
    forced = (jb == 0) | ((jb <= cur) & (jb > cur - N_LOCAL_SEL))
    visible = jb * SEL_BLOCK <= tq
    val = [jnp.where(visible, x + FORCE_BONUS * forced.astype(F32), NEG) for x in imp]
    sel = [jnp.zeros((LANES, Q_BLOCK), F32) for _ in groups]
    jbf = jb.astype(F32)
    for _ in range(top):
        mx = [jnp.max(x, axis=0, keepdims=True) for x in val]
        first = [jnp.min(jnp.where(val[g] == mx[g], jbf, float(LANES)), axis=0, keepdims=True) for g in groups]
        pick = [jbf == x for x in first]
        sel = [jnp.where(pick[g], 1.0, sel[g]) for g in groups]
        val = [jnp.where(pick[g], PICKED, val[g]) for g in groups]
    for g in groups:
        sel_ref[g * LANES:(g + 1) * LANES, :] = jnp.where(visible, sel[g], 0.0)


def _cmp_attention(qkv, kvc, overlap, top):
    s = qkv.shape[0]
    nc = kvc.shape[2]
    return pl.pallas_call(
        functools.partial(_cmp_kernel, nc=nc, top=top), grid=(s // Q_BLOCK,),
        in_specs=[pl.BlockSpec((Q_BLOCK, NSA_WIDTH), lambda i: (i, 0)),
                  pl.BlockSpec(kvc.shape, lambda i: (0, 0, 0, 0)),
                  pl.BlockSpec((LANES, nc), lambda i: (0, 0))],
        out_specs=[pl.BlockSpec((Q_BLOCK, NSA_WIDTH), lambda i: (i, 0)),
                   pl.BlockSpec((NSA_GROUPS * LANES, Q_BLOCK), lambda i: (0, i))],
        out_shape=[jax.ShapeDtypeStruct((s, NSA_WIDTH), F32),
                   jax.ShapeDtypeStruct((NSA_GROUPS * LANES, s), F32)],
        compiler_params=_params(1), name="cmp_attn")(qkv, kvc, overlap)


def _slc_kernel(q_ref, k_ref, v_ref, sel_ref, o_ref, m_ref, l_ref, acc_ref, *, tk):
    i = pl.program_id(1)
    rows = HPG * Q_BLOCK
    q = _stack_heads(q_ref)
    sel = sel_ref[...].T.astype(BF16)
    qpos = i * Q_BLOCK + _iota((Q_BLOCK, 1), 0)
    blk_off = _iota((LANES, tk), 0) - jnp.right_shift(_iota((LANES, tk), 1), 6)
    blocks_per_tile = tk // SEL_BLOCK
    m_ref[...] = jnp.full(m_ref.shape, NEG, F32)
    l_ref[...] = jnp.zeros(l_ref.shape, F32)
    acc_ref[...] = jnp.zeros(acc_ref.shape, F32)
    last = lax.div(i * Q_BLOCK + Q_BLOCK - 1, tk)

    def tile(kt, causal):
        start = pl.multiple_of(kt * tk, tk)
        expand = jnp.where(blk_off == kt * blocks_per_tile, 1.0, 0.0).astype(BF16)
        chosen = _dot(sel, expand)
        if causal:
            chosen = jnp.where(start + _iota((1, tk), 1) <= qpos, chosen, 0.0)
        s = _dot_nt(q, k_ref[pl.ds(start, tk), :]).reshape(HPG, Q_BLOCK, tk)
        s = jnp.where((chosen > 0.5)[None], s, NEG)
        cols = [s[:, :, c * LANES:(c + 1) * LANES] for c in range(tk // LANES)]
        m_old = m_ref[...]
        m_new = jnp.maximum(m_old, jnp.max(functools.reduce(jnp.maximum, cols), axis=-1, keepdims=True))
        alpha = jnp.exp2(m_old - m_new)
        ps = [jnp.exp2(col - m_new) for col in cols]
        m_ref[...] = m_new
        l_ref[...] = alpha * l_ref[...] + functools.reduce(jnp.add, ps)
        p = jnp.concatenate(ps, axis=-1).reshape(rows, tk).astype(BF16)
        acc_ref[...] = alpha.reshape(rows, LANES) * acc_ref[...] + _dot(p, v_ref[pl.ds(start, tk), :])

    def body(j, carry):
        tile(2 * j, False)
        tile(2 * j + 1, False)
        return carry

    lax.fori_loop(0, lax.div(last, 2), body, 0)

    @pl.when(lax.rem(last, 2) == 1)
    def _():
        tile(last - 1, False)

    tile(last, True)
    m = m_ref[...].reshape(rows, LANES)
    l = jnp.sum(l_ref[...], axis=-1, keepdims=True).reshape(rows, 1)
    _store_heads(o_ref, jnp.where(m > 0.5 * NEG, acc_ref[...] / l, 0.0))


def _slc_attention(qkv, sel, tk=512):
    s = qkv.shape[0]
    gw = HPG * HEAD_DIM
    return pl.pallas_call(
        functools.partial(_slc_kernel, tk=tk), grid=(NSA_GROUPS, s // Q_BLOCK),
        in_specs=[pl.BlockSpec((Q_BLOCK, gw), lambda g, i: (i, g)),
                  pl.BlockSpec((s, HEAD_DIM), lambda g, i: (0, KS_BLOCK + g)),
                  pl.BlockSpec((s, HEAD_DIM), lambda g, i: (0, VS_BLOCK + g)),
                  pl.BlockSpec((LANES, Q_BLOCK), lambda g, i: (g, i))],
        out_specs=pl.BlockSpec((Q_BLOCK, gw), lambda g, i: (i, g)),
        out_shape=jax.ShapeDtypeStruct((s, NSA_WIDTH), F32),
        scratch_shapes=[pltpu.VMEM((HPG, Q_BLOCK, LANES), F32), pltpu.VMEM((HPG, Q_BLOCK, LANES), F32),
                        pltpu.VMEM((HPG * Q_BLOCK, HEAD_DIM), F32)],
        compiler_params=_params(2), name="slc_attn")(qkv, qkv, qkv, sel)


def _win_kernel(q_ref, k_ref, v_ref, o_ref):
    i = pl.program_id(1)
    band = WINDOW + Q_BLOCK
    start = pl.multiple_of(jnp.maximum(i * Q_BLOCK - WINDOW, 0), Q_BLOCK)
    k_b = k_ref[pl.ds(start, band), :]
    v_b = v_ref[pl.ds(start, band), :]
    qpos = i * Q_BLOCK + _iota((Q_BLOCK, 1), 0)
    kpos = start + _iota((1, band), 1)
    allowed = ((kpos <= qpos) & (kpos > qpos - WINDOW))[None]
    s = _dot_nt(_stack_heads(q_ref), k_b).reshape(HPG, Q_BLOCK, band)
    s = jnp.where(allowed, s, NEG)
    e = jnp.exp2(s - jnp.max(s, axis=-1, keepdims=True))
    p = e / jnp.sum(e, axis=-1, keepdims=True)
    _store_heads(o_ref, _dot(p.reshape(HPG * Q_BLOCK, band).astype(BF16), v_b))


def _win_attention(qkv):
    s = qkv.shape[0]
    gw = HPG * HEAD_DIM
    return pl.pallas_call(
        _win_kernel, grid=(NSA_GROUPS, s // Q_BLOCK),
        in_specs=[pl.BlockSpec((Q_BLOCK, gw), lambda g, i: (i, g)),
                  pl.BlockSpec((s, HEAD_DIM), lambda g, i: (0, KW_BLOCK + g)),
                  pl.BlockSpec((s, HEAD_DIM), lambda g, i: (0, VW_BLOCK + g))],
        out_specs=pl.BlockSpec((Q_BLOCK, gw), lambda g, i: (i, g)),
        out_shape=jax.ShapeDtypeStruct((s, NSA_WIDTH), F32),
        compiler_params=_params(2), name="win_attn")(qkv, qkv, qkv)


def _head_sum(x):
    seg = (jnp.right_shift(_iota((LANES, LANES), 0), 6) ==
           jnp.right_shift(_iota((LANES, LANES), 1), 6)).astype(F32)
    return _dot_by_01(x, seg)


def _rwkv_prep_kernel(z_ref, zp_ref, mu_ref, w0_ref, wup_ref, a0_ref, aup_ref, gup_ref, kscale_ref, ka_ref,
                      r_out, lw_out, k_out, v_out, kk_out, a_out, g_out):
    i = pl.program_id(0)
    z = z_ref[...]
    tm = z.shape[0]
    last = jnp.where(i == 0, 0.0, zp_ref[7:8, :])
    z_prev = jnp.where(_iota((tm, 1), 0) == 0, last, pltpu.roll(z, 1, 0))
    z = z + (z_prev - z) * mu_ref[...]
    w = RWKV_WIDTH
    r, k, v = z[:, 0:w], z[:, w:2 * w], z[:, 2 * w:3 * w]
    wd = z[:, 3 * w:3 * w + LORA_PAD]
    ad = z[:, 3 * w + LORA_PAD:3 * w + 2 * LORA_PAD]
    gd = z[:, 3 * w + 2 * LORA_PAD:]
    wlog = -jax.nn.softplus(-(w0_ref[...] + _dot_split(jnp.tanh(wd), wup_ref[...]))) - 0.5
    a = jax.nn.sigmoid(a0_ref[...] + _dot(ad.astype(BF16), aup_ref[...].astype(BF16)))
    g = _dot(jax.nn.sigmoid(gd).astype(BF16), gup_ref[...].astype(BF16))
    kk = k * kscale_ref[...]
    for b in range(w // LANES):
        sl = slice(b * LANES, (b + 1) * LANES)
        kb = kk[:, sl]
        kk_out[:, sl] = kb * lax.rsqrt(jnp.maximum(_head_sum(kb * kb), 1e-24))
    r_out[...] = r
    lw_out[...] = -jnp.exp(wlog)
    k_out[...] = k * (1.0 + (a - 1.0) * ka_ref[...])
    v_out[...] = v
    a_out[...] = a
    g_out[...] = g


def _rwkv_prep(z, mu, w0, w_up, a0, a_up, g_up, k_k, k_a, tm=256):
    s = z.shape[0]
    w = RWKV_WIDTH
    row = lambda n: pl.BlockSpec((1, n), lambda i: (0, 0))
    full = lambda a: pl.BlockSpec(a.shape, lambda i: (0, 0))
    out = pl.BlockSpec((tm, w), lambda i: (i, 0))
    return pl.pallas_call(
        _rwkv_prep_kernel, grid=(s // tm,),
        in_specs=[pl.BlockSpec((tm, RWKV_Z), lambda i: (i, 0)),
                  pl.BlockSpec((8, RWKV_Z), lambda i: (jnp.maximum(i * (tm // 8) - 1, 0), 0)),
                  row(RWKV_Z), row(w), full(w_up), row(w), full(a_up), full(g_up), row(w), row(w)],
        out_specs=[out] * 7,
        out_shape=[jax.ShapeDtypeStruct((s, w), F32)] * 7,
        compiler_params=_params(1), name="rwkv_prep")(z, z, mu, w0, w_up, a0, a_up, g_up, k_k, k_a)


def _scan_kernel(r_ref, lw_ref, k_ref, v_ref, kk_ref, a_ref, g_ref, rk_ref, lnw_ref, lnb_ref,
                 o_ref, state_ref):
    c = SCAN_CHUNK

    @pl.when(pl.program_id(0) == 0)
    def _():
        state_ref[...] = jnp.zeros_like(state_ref)

    tri = (_iota((c, c), 0) >= _iota((c, c), 1)).astype(F32)
    head0 = _iota((1, LANES), 1) < RWKV_HEAD
    tok_r = jnp.bitwise_and(_iota((LANES, LANES), 0), c - 1)
    tok_c = jnp.bitwise_and(_iota((LANES, LANES), 1), c - 1)
    strict = tok_r > tok_c
    incl = tok_r >= tok_c
    eye = (_iota((LANES, LANES), 0) == _iota((LANES, LANES), 1)).astype(F32)

    def stack(x):
        return jnp.concatenate([jnp.where(head0, x, 0.0), jnp.where(head0, 0.0, x)], axis=0)

    pairs = range(RWKV_WIDTH // LANES)
    sls = [slice(p * LANES, (p + 1) * LANES) for p in pairs]
    lw = [lw_ref[:, sl] for sl in sls]
    cum = [_dot_01_by(tri, x) for x in lw]
    total = [x[c - 1:c, :] for x in cum]
    r = [r_ref[:, sl] for sl in sls]
    k = [k_ref[:, sl] for sl in sls]
    v = [v_ref[:, sl] for sl in sls]
    kk = [kk_ref[:, sl] for sl in sls]
    b = [kk[p] * a_ref[:, sls[p]] for p in pairs]
    e_neg = [jnp.exp(-x) for x in cum]
    a2b = [stack(-kk[p] * jnp.exp(cum[p] - lw[p])).astype(BF16) for p in pairs]
    r2b = [stack(r[p] * jnp.exp(cum[p])).astype(BF16) for p in pairs]
    b2 = [stack(b[p] * e_neg[p]) for p in pairs]
    k2 = [stack(k[p] * e_neg[p]) for p in pairs]
    v2 = [stack(x).astype(BF16) for x in v]
    gram = [_dot_nt(jnp.concatenate([a2b[p], r2b[p]], axis=0),
                    jnp.concatenate([b2[p], k2[p]], axis=0).astype(BF16)) for p in pairs]
    l_ab = [jnp.where(strict, x[0:LANES, 0:LANES], 0.0) for x in gram]
    m_ak = [jnp.where(strict, x[0:LANES, LANES:], 0.0).astype(BF16) for x in gram]
    m_rb = [jnp.where(incl, x[LANES:, 0:LANES], 0.0).astype(BF16) for x in gram]
    m_rk = [jnp.where(incl, x[LANES:, LANES:], 0.0).astype(BF16) for x in gram]
    inv = [eye + x for x in l_ab]
    pw = l_ab
    for _ in range(5):
        pwb = [x.astype(BF16) for x in pw]
        pw = [_dot(x, x) for x in pwb]
        inv = [inv[p] + _dot(inv[p].astype(BF16), pw[p].astype(BF16)) for p in pairs]
    state = [state_ref[p] for p in pairs]
    sb = [x.astype(BF16) for x in state]
    x = [_dot_nt(a2b[p], sb[p]) + _dot(m_ak[p], v2[p]) for p in pairs]
    u2 = [_dot(inv[p].astype(BF16), x[p].astype(BF16)).astype(BF16) for p in pairs]
    y2 = [_dot_nt(r2b[p], sb[p]) + _dot(m_rb[p], u2[p]) + _dot(m_rk[p], v2[p]) for p in pairs]
    for p in pairs:
        e_end = jnp.exp(total[p] - cum[p])
        uv = jnp.concatenate([u2[p], v2[p]], axis=0)
        bk = jnp.concatenate([stack(b[p] * e_end), stack(k[p] * e_end)], axis=0).astype(BF16)
        state_ref[p] = state[p] * jnp.exp(total[p]) + _dot_tn(uv, bk)
    y = [x[0:c] + x[c:] for x in y2]
    mean = [_head_sum(x) * (1.0 / RWKV_HEAD) for x in y]
    d = [y[p] - mean[p] for p in pairs]
    var = [_head_sum(x * x) * (1.0 / RWKV_HEAD) for x in d]
    bonus = [_head_sum(r[p] * k[p] * rk_ref[:, sls[p]]) for p in pairs]
    for p in pairs:
        sl = sls[p]
        out = d[p] * lax.rsqrt(var[p] + GN_EPS) * lnw_ref[:, sl] + lnb_ref[:, sl]
        o_ref[:, sl] = (out + bonus[p] * v[p]) * g_ref[:, sl]


def _rwkv_scan(r, lw, k, v, kk, a, g, r_k, ln_w, ln_b):
    s, w = r.shape
    blk = pl.BlockSpec((SCAN_CHUNK, w), lambda c: (c, 0))
    row = pl.BlockSpec((1, w), lambda c: (0, 0))
    return pl.pallas_call(
        _scan_kernel, grid=(s // SCAN_CHUNK,),
        in_specs=[blk] * 7 + [row] * 3,
        out_specs=blk,
        out_shape=jax.ShapeDtypeStruct((s, w), F32),
        scratch_shapes=[pltpu.VMEM((w // LANES, LANES, LANES), F32)],
        compiler_params=_params(1), name="rwkv_scan")(r, lw, k, v, kk, a, g, r_k, ln_w, ln_b)


def _merge_kernel(oc_ref, os_ref, ow_ref, bg_ref, ob_ref, mg_ref, pa_ref, pb_ref, o_ref):
    bg = jax.nn.sigmoid(bg_ref[...])
    heads = []
    for h in range(NSA_HEADS):
        sl = slice(h * HEAD_DIM, (h + 1) * HEAD_DIM)
        c0 = NSA_BRANCHES * h
        heads.append(bg[:, c0:c0 + 1] * oc_ref[:, sl] + bg[:, c0 + 1:c0 + 2] * os_ref[:, sl]
                     + bg[:, c0 + 2:c0 + 3] * ow_ref[:, sl])
    o_a = jnp.concatenate(heads, axis=-1).astype(BF16)
    y_a = _dot(o_a, pa_ref[...])
    y_b = _dot(ob_ref[...].astype(BF16), pb_ref[...])
    mg = jax.nn.sigmoid(mg_ref[...].astype(F32))
    o_ref[...] = (mg[:, :D_MODEL] * y_a + mg[:, D_MODEL:] * y_b).astype(o_ref.dtype)


def _merge(o_cmp, o_slc, o_win, bgate, o_b, mgate, proj_a, proj_b, tm=256):
    s = o_cmp.shape[0]
    half = lambda: pl.BlockSpec((tm, NSA_WIDTH), lambda i: (i, 0))
    wspec = pl.BlockSpec((NSA_WIDTH, D_MODEL), lambda i: (0, 0))
    return pl.pallas_call(
        _merge_kernel, grid=(s // tm,),
        in_specs=[half(), half(), half(), pl.BlockSpec((tm, LANES), lambda i: (i, 0)), half(),
                  pl.BlockSpec((tm, 2 * D_MODEL), lambda i: (i, 0)), wspec, wspec],
        out_specs=pl.BlockSpec((tm, D_MODEL), lambda i: (i, 0)),
        out_shape=jax.ShapeDtypeStruct((s, D_MODEL), BF16),
        compiler_params=_params(1), name="merge")(o_cmp, o_slc, o_win, bgate, o_b, mgate, proj_a, proj_b)


def _outproj_kernel(x_ref, m_ref, w_ref, g_ref, x1_ref, h_ref):
    x1 = x_ref[...] + _dot(m_ref[...], w_ref[...])
    x1_ref[...] = x1
    y = x1 * lax.rsqrt(jnp.mean(x1 * x1, axis=-1, keepdims=True) + EPS)
    h_ref[...] = (y * g_ref[...]).astype(h_ref.dtype)


def _outproj(x, mixed, w_out, g, tm=256):
    s, d = x.shape
    blk = pl.BlockSpec((tm, d), lambda i: (i, 0))
    return pl.pallas_call(
        _outproj_kernel, grid=(s // tm,),
        in_specs=[blk, blk, pl.BlockSpec((d, d), lambda i: (0, 0)), pl.BlockSpec((1, d), lambda i: (0, 0))],
        out_specs=[blk, blk],
        out_shape=[jax.ShapeDtypeStruct((s, d), F32), jax.ShapeDtypeStruct((s, d), BF16)],
        compiler_params=_params(1), name="outproj")(x, mixed, w_out, g.reshape(1, d))


def _ffn_up_kernel(h_ref, wg_ref, wv_ref, cwg_ref, cwv_ref, cbg_ref, cbv_ref, o_ref, ug_ref, uv_ref):
    tm = h_ref.shape[0]

    @pl.when(pl.program_id(1) == 0)
    def _():
        ug_ref[0:8, :] = jnp.zeros((8, ug_ref.shape[1]), F32)
        uv_ref[0:8, :] = jnp.zeros((8, uv_ref.shape[1]), F32)

    h = h_ref[...]

    def conv(u_ref, w_ref, cw_ref, cb_ref):
        u_ref[8:8 + tm, :] = _dot(h, w_ref[...])
        out = cb_ref[...]
        for tap in range(CONV_WIDTH):
            off = 8 - (CONV_WIDTH - 1 - tap)
            out = out + u_ref[off:off + tm, :] * cw_ref[tap:tap + 1, :]
        u_ref[0:8, :] = u_ref[tm:tm + 8, :]
        return out

    gate = conv(ug_ref, wg_ref, cwg_ref, cbg_ref)
    val = conv(uv_ref, wv_ref, cwv_ref, cbv_ref)
    o_ref[...] = (jax.nn.silu(gate) * val).astype(o_ref.dtype)


def _ffn_up(h, w_up, conv_w, conv_b, tm=512, tn=512):
    s, d = h.shape
    nj = D_FF // tn
    return pl.pallas_call(
        _ffn_up_kernel, grid=(nj, s // tm),
        in_specs=[pl.BlockSpec((tm, d), lambda j, i: (i, 0)),
                  pl.BlockSpec((d, tn), lambda j, i: (0, j)),
                  pl.BlockSpec((d, tn), lambda j, i: (0, nj + j)),
                  pl.BlockSpec((CONV_WIDTH, tn), lambda j, i: (0, j)),
                  pl.BlockSpec((CONV_WIDTH, tn), lambda j, i: (0, nj + j)),
                  pl.BlockSpec((1, tn), lambda j, i: (0, j)),
                  pl.BlockSpec((1, tn), lambda j, i: (0, nj + j))],
        out_specs=pl.BlockSpec((tm, tn), lambda j, i: (i, j)),
        out_shape=jax.ShapeDtypeStruct((s, D_FF), BF16),
        scratch_shapes=[pltpu.VMEM((tm + 8, tn), F32), pltpu.VMEM((tm + 8, tn), F32)],
        compiler_params=_params(2), name="ffn_up")(h, w_up, w_up, conv_w, conv_w, conv_b, conv_b)


def _ffn_down_kernel(a_ref, w_ref, x_ref, g_ref, o_ref, y_ref):
    n = pl.program_id(1)
    y_ref[n] = x_ref[...] + _dot(a_ref[...], w_ref[...])

    @pl.when(n == pl.num_programs(1) - 1)
    def _():
        x2 = jnp.concatenate([y_ref[c] for c in range(y_ref.shape[0])], axis=-1)
        y = x2 * lax.rsqrt(jnp.mean(x2 * x2, axis=-1, keepdims=True) + EPS)
        o_ref[...] = y * g_ref[...]


def _ffn_down(act, w_down, x1, g, tm=512, tn=512):
    s, d = x1.shape
    return pl.pallas_call(
        _ffn_down_kernel, grid=(s // tm, d // tn),
        in_specs=[pl.BlockSpec((tm, D_FF), lambda i, n: (i, 0)),
                  pl.BlockSpec((D_FF, tn), lambda i, n: (0, n)),
                  pl.BlockSpec((tm, tn), lambda i, n: (i, n)),
                  pl.BlockSpec((1, d), lambda i, n: (0, 0))],
        out_specs=pl.BlockSpec((tm, d), lambda i, n: (i, 0)),
        out_shape=jax.ShapeDtypeStruct((s, d), F32),
        scratch_shapes=[pltpu.VMEM((d // tn, tm, tn), F32)],
        compiler_params=_params(2), name="ffn_down")(act, w_down, x1, g.reshape(1, d))


def _pad_cols(w, n):
    return jnp.pad(w, ((0, 0), (0, n - w.shape[1])))


def _pad_rows(w, n):
    return jnp.pad(w, ((0, n - w.shape[0]), (0, 0)))


def _split_rwkv_cols(t):
    w = RWKV_WIDTH
    rkv, wd, ad, gd = (t[:, :3 * w], t[:, 3 * w:3 * w + DECAY_LORA],
                       t[:, 3 * w + DECAY_LORA:3 * w + DECAY_LORA + ICLR_LORA],
                       t[:, 3 * w + DECAY_LORA + ICLR_LORA:])
    return jnp.concatenate([rkv, _pad_cols(wd, LORA_PAD), _pad_cols(ad, LORA_PAD), gd], axis=1)


def _rope_tables(seq):
    half = HEAD_DIM // 2
    inv = ROPE_THETA ** (-jnp.arange(half, dtype=F32) / half)
    ang = jnp.arange(seq, dtype=F32)[:, None] * inv[None, :]
    cos, sin = jnp.cos(ang), jnp.sin(ang)
    return jnp.concatenate([cos, cos], axis=1), jnp.concatenate([-sin, sin], axis=1)


def _overlap_matrix(nc):
    c_start = jnp.arange(nc) * CMP_STRIDE
    s_start = jnp.arange(LANES) * SEL_BLOCK
    ov = jnp.clip(jnp.minimum(c_start[None, :] + CMP_BLOCK, s_start[:, None] + SEL_BLOCK)
                  - jnp.maximum(c_start[None, :], s_start[:, None]), 0)
    return ov.astype(F32) / CMP_BLOCK


def _layer(x, norm1_g, w_in, cmp_k_pe, cmp_k_w1, cmp_k_w2, cmp_v_pe, cmp_v_w1, cmp_v_w2, rwkv_mu, rwkv_w0,
           rwkv_w_up, rwkv_a0, rwkv_a_up, rwkv_g_up, rwkv_k_k, rwkv_k_a, rwkv_r_k, rwkv_ln_w, rwkv_ln_b,
           proj_nsa, proj_rwkv, w_out, norm2_g, ffn_up, ffn_conv_w, ffn_conv_b, ffn_down, final_g):
    seq = x.shape[0]
    n_sel = seq // SEL_BLOCK
    assert seq % 2048 == 0 and n_sel <= LANES
    nc = seq // CMP_STRIDE
    w = RWKV_WIDTH

    c_gate = QKV_WIDTH
    c_z = c_gate + NSA_HEADS * NSA_BRANCHES
    c_merge = c_z + 3 * w + DECAY_LORA + ICLR_LORA + GATE_LORA
    w_qkv = w_in[:, :c_gate].astype(BF16)
    w_bgate = _pad_cols(w_in[:, c_gate:c_z], LANES).astype(BF16)
    w_z = _split_rwkv_cols(w_in[:, c_z:c_merge]).astype(BF16)
    w_merge = w_in[:, c_merge:].astype(BF16)

    h = _rmsnorm(x, norm1_g, BF16)
    qkv = _matmul(h, w_qkv, F32, 1024, 1280, "proj_qkv")
    bgate = _matmul(h, w_bgate, F32, 1024, LANES, "proj_bgate")
    z = _matmul(h, w_z, F32, 1024, 896, "proj_rwkv")
    mgate = _matmul(h, w_merge, BF16, 1024, 1024, "proj_merge")

    cosf, sinf = _rope_tables(seq)
    qkv = _rope(qkv, cosf, sinf)
    kv_c = qkv[:, NSA_WIDTH:NSA_WIDTH + 2 * KV_WIDTH]
    tok = kv_c.reshape(nc, CMP_STRIDE, 2, NSA_GROUPS, HEAD_DIM)
    tok = jnp.transpose(tok, (2, 3, 0, 1, 4)).reshape(2, NSA_GROUPS, nc, CMP_STRIDE * HEAD_DIM)
    tok_next = jnp.concatenate([tok[:, :, 1:], jnp.zeros_like(tok[:, :, :1])], axis=2)
    pe = jnp.stack([cmp_k_pe, cmp_v_pe]).reshape(2, 1, CMP_BLOCK * HEAD_DIM)
    kv_cmp = _compress(tok, tok_next, pe, jnp.stack([cmp_k_w1, cmp_v_w1]), jnp.stack([cmp_k_w2, cmp_v_w2]))
    o_cmp, sel = _cmp_attention(qkv, kv_cmp, _overlap_matrix(nc), min(SEL_TOPK, n_sel))
    o_slc = _slc_attention(qkv, sel)
    o_win = _win_attention(qkv)

    mu = _split_rwkv_cols(rwkv_mu.reshape(1, -1))
    r, lw, k, v, kk, a, g = _rwkv_prep(
        z, mu, rwkv_w0.reshape(1, w), _pad_rows(rwkv_w_up, LORA_PAD), rwkv_a0.reshape(1, w),
        _pad_rows(rwkv_a_up, LORA_PAD), rwkv_g_up, rwkv_k_k.reshape(1, w), rwkv_k_a.reshape(1, w))
    o_b = _rwkv_scan(r, lw, k, v, kk, a, g, rwkv_r_k.reshape(1, w), rwkv_ln_w.reshape(1, w),
                     rwkv_ln_b.reshape(1, w))

    mixed = _merge(o_cmp, o_slc, o_win, bgate, o_b, mgate, proj_nsa.astype(BF16), proj_rwkv.astype(BF16))
    x1, h2 = _outproj(x, mixed, w_out.astype(BF16), norm2_g)
    act = _ffn_up(h2, ffn_up.astype(BF16), ffn_conv_w, ffn_conv_b.reshape(1, -1))
    return _ffn_down(act, ffn_down.astype(BF16), x1, final_g)


def kernel(x, norm1_g, w_in, cmp_k_pe, cmp_k_w1, cmp_k_w2, cmp_v_pe, cmp_v_w1, cmp_v_w2, rwkv_mu, rwkv_w0,
           rwkv_w_up, rwkv_a0, rwkv_a_up, rwkv_g_up, rwkv_k_k, rwkv_k_a, rwkv_r_k, rwkv_ln_w, rwkv_ln_b,
           proj_nsa, proj_rwkv, w_out, norm2_g, ffn_up, ffn_conv_w, ffn_conv_b, ffn_down, final_g):
    assert norm1_g.shape[0] == 1, "single-layer block: the final rmsnorm is fused into the down projection"
    layer = (norm1_g, w_in, cmp_k_pe, cmp_k_w1, cmp_k_w2, cmp_v_pe, cmp_v_w1, cmp_v_w2, rwkv_mu, rwkv_w0,
             rwkv_w_up, rwkv_a0, rwkv_a_up, rwkv_g_up, rwkv_k_k, rwkv_k_a, rwkv_r_k, rwkv_ln_w, rwkv_ln_b,
             proj_nsa, proj_rwkv, w_out, norm2_g, ffn_up, ffn_conv_w, ffn_conv_b, ffn_down)
    weights = [t.reshape(t.shape[1:]) for t in layer]
    batch, seq, d = x.shape
    if batch == 1:
        return _layer(x.reshape(seq, d), *weights, final_g).reshape(1, seq, d)
    return jnp.stack([_layer(x[b], *weights, final_g) for b in range(batch)])
```

```python
import functools

import jax
import jax.numpy as jnp
from jax import lax
from jax.experimental import pallas as pl
from jax.experimental.pallas import tpu as pltpu

F32 = jnp.float32
BF16 = jnp.bfloat16
HI = lax.Precision.HIGHEST

LANES = 128
D_MODEL = 2048
HEAD_DIM = 128
NSA_WIDTH = D_MODEL // 2
NSA_HEADS = NSA_WIDTH // HEAD_DIM
NSA_GROUPS = 2
HPG = NSA_HEADS // NSA_GROUPS
KV_WIDTH = NSA_GROUPS * HEAD_DIM
NSA_BRANCHES = 3
CMP_BLOCK = 32
CMP_STRIDE = 16
SEL_BLOCK = 64
SEL_TOPK = 16
N_LOCAL_SEL = 2
WINDOW = 512
Q_BLOCK = 128
ROPE_THETA = 10000.0
ATTN_SCALE = HEAD_DIM ** -0.5
QUERY_SCALE = ATTN_SCALE * 1.4426950408889634
FORCE_BONUS = 1.0e4
NEG = -1.0e30
PICKED = -3.0e38

RWKV_WIDTH = D_MODEL // 2
RWKV_HEAD = 64
DECAY_LORA = 96
ICLR_LORA = 96
GATE_LORA = 256
LORA_PAD = 128
RWKV_Z = 3 * RWKV_WIDTH + 2 * LORA_PAD + GATE_LORA
GN_EPS = 64e-5
SCAN_CHUNK = 64

D_FF = 5632
CONV_WIDTH = 3
EPS = 1e-6

QKV_WIDTH = NSA_WIDTH + 6 * KV_WIDTH
QKV_BLOCKS = QKV_WIDTH // LANES
ROPE_BLOCKS = tuple(range(0, 10)) + (12, 13, 16, 17)
KS_BLOCK, VS_BLOCK, KW_BLOCK, VW_BLOCK = 12, 14, 16, 18

VMEM_LIMIT = 56 * 1024 * 1024


def _params(n_axes):
    return pltpu.CompilerParams(dimension_semantics=("arbitrary",) * n_axes,
                                vmem_limit_bytes=VMEM_LIMIT)


def _dot(a, b, precision=None):
    return jnp.dot(a, b, preferred_element_type=F32, precision=precision)


def _dot_nt(a, b, precision=None):
    return lax.dot_general(a, b, (((1,), (1,)), ((), ())), preferred_element_type=F32,
                           precision=precision)


def _dot_tn(a, b, precision=None):
    return lax.dot_general(a, b, (((0,), (0,)), ((), ())), preferred_element_type=F32,
                           precision=precision)


def _split_bf16(x, parts):
    terms = []
    for _ in range(parts - 1):
        t = x.astype(BF16)
        terms.append(t)
        x = x - t.astype(F32)
    return terms + [x.astype(BF16)]


def _dot_by_01(x, m01):
    m = m01.astype(BF16)
    return functools.reduce(jnp.add, [_dot(t, m) for t in _split_bf16(x, 3)])


def _dot_01_by(m01, x):
    m = m01.astype(BF16)
    return functools.reduce(jnp.add, [_dot(m, t) for t in _split_bf16(x, 3)])


def _dot_split(x, w):
    x_hi, x_lo = _split_bf16(x, 2)
    w_hi, w_lo = _split_bf16(w, 2)
    return _dot(x_hi, w_hi) + (_dot(x_hi, w_lo) + _dot(x_lo, w_hi))


def _iota(shape, dim):
    return lax.broadcasted_iota(jnp.int32, shape, dim)


def _rmsnorm_kernel(x_ref, g_ref, o_ref):
    x = x_ref[...]
    y = x * lax.rsqrt(jnp.mean(x * x, axis=-1, keepdims=True) + EPS)
    o_ref[...] = (y * g_ref[...]).astype(o_ref.dtype)


def _rmsnorm(x, g, out_dtype, tm=512):
    s, d = x.shape
    return pl.pallas_call(
        _rmsnorm_kernel, grid=(s // tm,),
        in_specs=[pl.BlockSpec((tm, d), lambda i: (i, 0)), pl.BlockSpec((1, d), lambda i: (0, 0))],
        out_specs=pl.BlockSpec((tm, d), lambda i: (i, 0)),
        out_shape=jax.ShapeDtypeStruct((s, d), out_dtype),
        compiler_params=_params(1), name="rmsnorm")(x, g.reshape(1, d))


def _mm_kernel(a_ref, b_ref, o_ref):
    o_ref[...] = _dot(a_ref[...], b_ref[...]).astype(o_ref.dtype)


def _matmul(a, b, out_dtype, tm, tn, name):
    m, k = a.shape
    n = b.shape[1]
    return pl.pallas_call(
        _mm_kernel, grid=(m // tm, n // tn),
        in_specs=[pl.BlockSpec((tm, k), lambda i, j: (i, 0)), pl.BlockSpec((k, tn), lambda i, j: (0, j))],
        out_specs=pl.BlockSpec((tm, tn), lambda i, j: (i, j)),
        out_shape=jax.ShapeDtypeStruct((m, n), out_dtype),
        compiler_params=_params(2), name=name)(a, b)


def _rope_kernel(p_ref, cos_ref, sin_ref, o_ref):
    c = cos_ref[...]
    s = sin_ref[...]
    for b in range(QKV_BLOCKS):
        t = p_ref[:, b * LANES:(b + 1) * LANES]
        if b in ROPE_BLOCKS:
            t = t * c + pltpu.roll(t, HEAD_DIM // 2, 1) * s
        if b < NSA_HEADS:
            t = t * QUERY_SCALE
        o_ref[:, b * LANES:(b + 1) * LANES] = t.astype(o_ref.dtype)


def _rope(qkv, cosf, sinf, tm=256):
    s = qkv.shape[0]
    return pl.pallas_call(
        _rope_kernel, grid=(s // tm,),
        in_specs=[pl.BlockSpec((tm, QKV_WIDTH), lambda i: (i, 0)),
                  pl.BlockSpec((tm, HEAD_DIM), lambda i: (i, 0)),
                  pl.BlockSpec((tm, HEAD_DIM), lambda i: (i, 0))],
        out_specs=pl.BlockSpec((tm, QKV_WIDTH), lambda i: (i, 0)),
        out_shape=jax.ShapeDtypeStruct((s, QKV_WIDTH), BF16),
        compiler_params=_params(1), name="rope")(qkv, cosf, sinf)


def _compress_kernel(ta_ref, tb_ref, pe_ref, w1_ref, w2_ref, o_ref):
    half = CMP_STRIDE * HEAD_DIM
    w1 = w1_ref[0]
    w1b = w1.astype(BF16)
    pre = _dot(ta_ref[0, 0], w1b[:half]) + _dot(tb_ref[0, 0], w1b[half:])
    pe = jnp.broadcast_to(pe_ref[0], (8, CMP_BLOCK * HEAD_DIM))
    pre = pre + _dot(pe, w1, HI)[0:1]
    h = jax.nn.gelu(pre)
    o_ref[0, 0] = _dot(h.astype(BF16), w2_ref[0].astype(BF16))


def _compress(ta, tb, pe, w1, w2):
    _, g, nc, width = ta.shape
    tok = pl.BlockSpec((1, 1, nc, width), lambda k, gi: (k, gi, 0, 0))
    return pl.pallas_call(
        _compress_kernel, grid=(2, g),
        in_specs=[tok, tok,
                  pl.BlockSpec((1, 1, CMP_BLOCK * HEAD_DIM), lambda k, gi: (k, 0, 0)),
                  pl.BlockSpec((1, CMP_BLOCK * HEAD_DIM, HEAD_DIM), lambda k, gi: (k, 0, 0)),
                  pl.BlockSpec((1, HEAD_DIM, HEAD_DIM), lambda k, gi: (k, 0, 0))],
        out_specs=pl.BlockSpec((1, 1, nc, HEAD_DIM), lambda k, gi: (k, gi, 0, 0)),
        out_shape=jax.ShapeDtypeStruct((2, g, nc, HEAD_DIM), F32),
        compiler_params=_params(2), name="compress")(ta, tb, pe, w1, w2)


def _stack_heads(q_ref, g=0):
    c0 = g * HPG * HEAD_DIM
    return jnp.concatenate([q_ref[:, c0 + j * HEAD_DIM:c0 + (j + 1) * HEAD_DIM] for j in range(HPG)], axis=0)


def _store_heads(o_ref, o, g=0):
    c0 = g * HPG * HEAD_DIM
    for j in range(HPG):
        o_ref[:, c0 + j * HEAD_DIM:c0 + (j + 1) * HEAD_DIM] = o[j * Q_BLOCK:(j + 1) * Q_BLOCK].astype(o_ref.dtype)


def _cmp_kernel(q_ref, kvc_ref, ov_ref, o_ref, sel_ref, *, nc, top):
    i = pl.program_id(0)
    groups = range(NSA_GROUPS)
    t = i * Q_BLOCK + _iota((Q_BLOCK, 1), 0)
    c_end = _iota((1, nc), 1) * CMP_STRIDE + (CMP_BLOCK - 1)
    cmask = (c_end <= t)[None]
    imp = []
    for g in groups:
        kc = kvc_ref[0, g].astype(BF16)
        vc = kvc_ref[1, g].astype(BF16)
        s = _dot_nt(_stack_heads(q_ref, g), kc).reshape(HPG, Q_BLOCK, nc)
        s = jnp.where(cmask, s, NEG)
        e = jnp.exp2(s - jnp.max(s, axis=-1, keepdims=True))
        p = e / jnp.sum(e, axis=-1, keepdims=True)
        p = jnp.where(cmask, p, 0.0)
        _store_heads(o_ref, _dot(p.reshape(HPG * Q_BLOCK, nc).astype(BF16), vc), g)
        imp.append(functools.reduce(jnp.add, [_dot_nt(ov_ref[...].astype(BF16), term)
                                              for term in _split_bf16(jnp.sum(p, axis=0), 3)]))

    jb = _iota((LANES, 1), 0)
    tq = i * Q_BLOCK + _iota((1, Q_BLOCK), 1)
    cur = jnp.right_shift(tq, 6)
    forced = (jb == 0) | ((jb <= cur) & (jb > cur - N_LOCAL_SEL))
    visible = jb * SEL_BLOCK <= tq
    val = [jnp.where(visible, x + FORCE_BONUS * forced.astype(F32), NEG) for x in imp]
    sel = [jnp.zeros((LANES, Q_BLOCK), F32) for _ in groups]
    jbf = jb.astype(F32)
    for _ in range(top):
        mx = [jnp.max(x, axis=0, keepdims=True) for x in val]
        first = [jnp.min(jnp.where(val[g] == mx[g], jbf, float(LANES)), axis=0, keepdims=True) for g in groups]
        pick = [jbf == x for x in first]
        sel = [jnp.where(pick[g], 1.0, sel[g]) for g in groups]
        val = [jnp.where(pick[g], PICKED, val[g]) for g in groups]
    for g in groups:
        sel_ref[g * LANES:(g + 1) * LANES, :] = jnp.where(visible, sel[g], 0.0)


def _cmp_attention(qkv, kvc, overlap, top):
    s = qkv.shape[0]
    nc = kvc.shape[2]
    return pl.pallas_call(
        functools.partial(_cmp_kernel, nc=nc, top=top), grid=(s // Q_BLOCK,),
        in_specs=[pl.BlockSpec((Q_BLOCK, NSA_WIDTH), lambda i: (i, 0)),
                  pl.BlockSpec(kvc.shape, lambda i: (0, 0, 0, 0)),
                  pl.BlockSpec((LANES, nc), lambda i: (0, 0))],
        out_specs=[pl.BlockSpec((Q_BLOCK, NSA_WIDTH), lambda i: (i, 0)),
                   pl.BlockSpec((NSA_GROUPS * LANES, Q_BLOCK), lambda i: (0, i))],
        out_shape=[jax.ShapeDtypeStruct((s, NSA_WIDTH), F32),
                   jax.ShapeDtypeStruct((NSA_GROUPS * LANES, s), F32)],
        compiler_params=_params(1), name="cmp_attn")(qkv, kvc, overlap)


def _slc_kernel(q_ref, k_ref, v_ref, sel_ref, o_ref, m_ref, l_ref, acc_ref, *, tk):
    i = pl.program_id(0)
    groups = range(NSA_GROUPS)
    rows = HPG * Q_BLOCK
    q = [_stack_heads(q_ref, g) for g in groups]
    sel = [sel_ref[g * LANES:(g + 1) * LANES, :].T.astype(BF16) for g in groups]
    qpos = i * Q_BLOCK + _iota((Q_BLOCK, 1), 0)
    blk_off = _iota((LANES, tk), 0) - jnp.right_shift(_iota((LANES, tk), 1), 6)
    blocks_per_tile = tk // SEL_BLOCK
    m_ref[...] = jnp.full(m_ref.shape, NEG, F32)
    l_ref[...] = jnp.zeros(l_ref.shape, F32)
    acc_ref[...] = jnp.zeros(acc_ref.shape, F32)
    last = lax.div(i * Q_BLOCK + Q_BLOCK - 1, tk)

    def tile(kt, causal):
        start = pl.multiple_of(kt * tk, tk)
        kv = lambda ref, g: ref[pl.ds(start, tk), g * HEAD_DIM:(g + 1) * HEAD_DIM]
        expand = jnp.where(blk_off == kt * blocks_per_tile, 1.0, 0.0).astype(BF16)
        for g in groups:
            chosen = _dot(sel[g], expand)
            if causal:
                chosen = jnp.where(start + _iota((1, tk), 1) <= qpos, chosen, 0.0)
            s = _dot_nt(q[g], kv(k_ref, g)).reshape(HPG, Q_BLOCK, tk)
            s = jnp.where((chosen > 0.5)[None], s, NEG)
            cols = [s[:, :, c * LANES:(c + 1) * LANES] for c in range(tk // LANES)]
            m_old = m_ref[g]
            m_new = jnp.maximum(m_old, jnp.max(functools.reduce(jnp.maximum, cols), axis=-1, keepdims=True))
            alpha = jnp.exp2(m_old - m_new)
            ps = [jnp.exp2(col - m_new) for col in cols]
            m_ref[g] = m_new
            l_ref[g] = alpha * l_ref[g] + functools.reduce(jnp.add, ps)
            p = jnp.concatenate(ps, axis=-1).reshape(rows, tk).astype(BF16)
            acc_ref[g] = alpha.reshape(rows, LANES) * acc_ref[g] + _dot(p, kv(v_ref, g))

    def body(kt, carry):
        tile(kt, False)
        return carry

    lax.fori_loop(0, last, body, 0)
    tile(last, True)
    for g in groups:
        m = m_ref[g].reshape(rows, LANES)
        l = jnp.sum(l_ref[g], axis=-1, keepdims=True).reshape(rows, 1)
        _store_heads(o_ref, jnp.where(m > 0.5 * NEG, acc_ref[g] / l, 0.0), g)


def _slc_attention(qkv, sel, tk=512):
    s = qkv.shape[0]
    return pl.pallas_call(
        functools.partial(_slc_kernel, tk=tk), grid=(s // Q_BLOCK,),
        in_specs=[pl.BlockSpec((Q_BLOCK, NSA_WIDTH), lambda i: (i, 0)),
                  pl.BlockSpec((s, KV_WIDTH), lambda i: (0, KS_BLOCK * LANES // KV_WIDTH)),
                  pl.BlockSpec((s, KV_WIDTH), lambda i: (0, VS_BLOCK * LANES // KV_WIDTH)),
                  pl.BlockSpec((NSA_GROUPS * LANES, Q_BLOCK), lambda i: (0, i))],
        out_specs=pl.BlockSpec((Q_BLOCK, NSA_WIDTH), lambda i: (i, 0)),
        out_shape=jax.ShapeDtypeStruct((s, NSA_WIDTH), F32),
        scratch_shapes=[pltpu.VMEM((NSA_GROUPS, HPG, Q_BLOCK, LANES), F32),
                        pltpu.VMEM((NSA_GROUPS, HPG, Q_BLOCK, LANES), F32),
                        pltpu.VMEM((NSA_GROUPS, HPG * Q_BLOCK, HEAD_DIM), F32)],
        compiler_params=_params(1), name="slc_attn")(qkv, qkv, qkv, sel)


def _win_kernel(q_ref, k_ref, v_ref, o_ref):
    i = pl.program_id(0)
    groups = range(NSA_GROUPS)
    band = WINDOW + Q_BLOCK
    start = pl.multiple_of(jnp.maximum(i * Q_BLOCK - WINDOW, 0), Q_BLOCK)
    kv = lambda ref, g: ref[pl.ds(start, band), g * HEAD_DIM:(g + 1) * HEAD_DIM]
    qpos = i * Q_BLOCK + _iota((Q_BLOCK, 1), 0)
    kpos = start + _iota((1, band), 1)
    allowed = ((kpos <= qpos) & (kpos > qpos - WINDOW))[None]
    s = [_dot_nt(_stack_heads(q_ref, g), kv(k_ref, g)).reshape(HPG, Q_BLOCK, band) for g in groups]
    s = [jnp.where(allowed, x, NEG) for x in s]
    e = [jnp.exp2(x - jnp.max(x, axis=-1, keepdims=True)) for x in s]
    p = [x / jnp.sum(x, axis=-1, keepdims=True) for x in e]
    o = [_dot(p[g].reshape(HPG * Q_BLOCK, band).astype(BF16), kv(v_ref, g)) for g in groups]
    for g in groups:
        _store_heads(o_ref, o[g], g)


def _win_attention(qkv):
    s = qkv.shape[0]
    return pl.pallas_call(
        _win_kernel, grid=(s // Q_BLOCK,),
        in_specs=[pl.BlockSpec((Q_BLOCK, NSA_WIDTH), lambda i: (i, 0)),
                  pl.BlockSpec((s, KV_WIDTH), lambda i: (0, KW_BLOCK * LANES // KV_WIDTH)),
                  pl.BlockSpec((s, KV_WIDTH), lambda i: (0, VW_BLOCK * LANES // KV_WIDTH))],
        out_specs=pl.BlockSpec((Q_BLOCK, NSA_WIDTH), lambda i: (i, 0)),
        out_shape=jax.ShapeDtypeStruct((s, NSA_WIDTH), F32),
        compiler_params=_params(1), name="win_attn")(qkv, qkv, qkv)


def _head_sum(x):
    seg = (jnp.right_shift(_iota((LANES, LANES), 0), 6) ==
           jnp.right_shift(_iota((LANES, LANES), 1), 6)).astype(F32)
    return _dot_by_01(x, seg)


def _rwkv_prep_kernel(z_ref, zp_ref, mu_ref, w0_ref, wup_ref, a0_ref, aup_ref, gup_ref, kscale_ref, ka_ref,
                      r_out, lw_out, k_out, v_out, kk_out, a_out, g_out):
    i = pl.program_id(0)
    z = z_ref[...]
    tm = z.shape[0]
    last = jnp.where(i == 0, 0.0, zp_ref[7:8, :])
    z_prev = jnp.where(_iota((tm, 1), 0) == 0, last, pltpu.roll(z, 1, 0))
    z = z + (z_prev - z) * mu_ref[...]
    w = RWKV_WIDTH
    r, k, v = z[:, 0:w], z[:, w:2 * w], z[:, 2 * w:3 * w]
    wd = z[:, 3 * w:3 * w + LORA_PAD]
    ad = z[:, 3 * w + LORA_PAD:3 * w + 2 * LORA_PAD]
    gd = z[:, 3 * w + 2 * LORA_PAD:]
    wlog = -jax.nn.softplus(-(w0_ref[...] + _dot_split(jnp.tanh(wd), wup_ref[...]))) - 0.5
    a = jax.nn.sigmoid(a0_ref[...] + _dot(ad.astype(BF16), aup_ref[...].astype(BF16)))
    g = _dot(jax.nn.sigmoid(gd).astype(BF16), gup_ref[...].astype(BF16))
    kk = k * kscale_ref[...]
    for b in range(w // LANES):
        sl = slice(b * LANES, (b + 1) * LANES)
        kb = kk[:, sl]
        kk_out[:, sl] = kb * lax.rsqrt(jnp.maximum(_head_sum(kb * kb), 1e-24))
    r_out[...] = r
    lw_out[...] = -jnp.exp(wlog)
    k_out[...] = k * (1.0 + (a - 1.0) * ka_ref[...])
    v_out[...] = v
    a_out[...] = a
    g_out[...] = g


def _rwkv_prep(z, mu, w0, w_up, a0, a_up, g_up, k_k, k_a, tm=256):
    s = z.shape[0]
    w = RWKV_WIDTH
    row = lambda n: pl.BlockSpec((1, n), lambda i: (0, 0))
    full = lambda a: pl.BlockSpec(a.shape, lambda i: (0, 0))
    out = pl.BlockSpec((tm, w), lambda i: (i, 0))
    return pl.pallas_call(
        _rwkv_prep_kernel, grid=(s // tm,),
        in_specs=[pl.BlockSpec((tm, RWKV_Z), lambda i: (i, 0)),
                  pl.BlockSpec((8, RWKV_Z), lambda i: (jnp.maximum(i * (tm // 8) - 1, 0), 0)),
                  row(RWKV_Z), row(w), full(w_up), row(w), full(a_up), full(g_up), row(w), row(w)],
        out_specs=[out] * 7,
        out_shape=[jax.ShapeDtypeStruct((s, w), F32)] * 7,
        compiler_params=_params(1), name="rwkv_prep")(z, z, mu, w0, w_up, a0, a_up, g_up, k_k, k_a)


def _scan_kernel(r_ref, lw_ref, k_ref, v_ref, kk_ref, a_ref, g_ref, rk_ref, lnw_ref, lnb_ref,
                 o_ref, state_ref):
    c = SCAN_CHUNK

    @pl.when(pl.program_id(0) == 0)
    def _():
        state_ref[...] = jnp.zeros_like(state_ref)

    tri = (_iota((c, c), 0) >= _iota((c, c), 1)).astype(F32)
    head0 = _iota((1, LANES), 1) < RWKV_HEAD
    tok_r = jnp.bitwise_and(_iota((LANES, LANES), 0), c - 1)
    tok_c = jnp.bitwise_and(_iota((LANES, LANES), 1), c - 1)
    strict = tok_r > tok_c
    incl = tok_r >= tok_c
    eye = (_iota((LANES, LANES), 0) == _iota((LANES, LANES), 1)).astype(F32)

    def stack(x):
        return jnp.concatenate([jnp.where(head0, x, 0.0), jnp.where(head0, 0.0, x)], axis=0)

    pairs = range(RWKV_WIDTH // LANES)
    sls = [slice(p * LANES, (p + 1) * LANES) for p in pairs]
    lw = [lw_ref[:, sl] for sl in sls]
    cum = [_dot_01_by(tri, x) for x in lw]
    total = [x[c - 1:c, :] for x in cum]
    r = [r_ref[:, sl] for sl in sls]
    k = [k_ref[:, sl] for sl in sls]
    v = [v_ref[:, sl] for sl in sls]
    kk = [kk_ref[:, sl] for sl in sls]
    b = [kk[p] * a_ref[:, sls[p]] for p in pairs]
    e_neg = [jnp.exp(-x) for x in cum]
    a2b = [stack(-kk[p] * jnp.exp(cum[p] - lw[p])).astype(BF16) for p in pairs]
    r2b = [stack(r[p] * jnp.exp(cum[p])).astype(BF16) for p in pairs]
    b2 = [stack(b[p] * e_neg[p]) for p in pairs]
    k2 = [stack(k[p] * e_neg[p]) for p in pairs]
    v2 = [stack(x).astype(BF16) for x in v]
    gram = [_dot_nt(jnp.concatenate([a2b[p], r2b[p]], axis=0),
                    jnp.concatenate([b2[p], k2[p]], axis=0).astype(BF16)) for p in pairs]
    l_ab = [jnp.where(strict, x[0:LANES, 0:LANES], 0.0) for x in gram]
    m_ak = [jnp.where(strict, x[0:LANES, LANES:], 0.0).astype(BF16) for x in gram]
    m_rb = [jnp.where(incl, x[LANES:, 0:LANES], 0.0).astype(BF16) for x in gram]
    m_rk = [jnp.where(incl, x[LANES:, LANES:], 0.0).astype(BF16) for x in gram]
    inv = [eye + x for x in l_ab]
    pw = l_ab
    for _ in range(5):
        pwb = [x.astype(BF16) for x in pw]
        pw = [_dot(x, x) for x in pwb]
        inv = [inv[p] + _dot(inv[p].astype(BF16), pw[p].astype(BF16)) for p in pairs]
    state = [state_ref[p] for p in pairs]
    sb = [x.astype(BF16) for x in state]
    x = [_dot_nt(a2b[p], sb[p]) + _dot(m_ak[p], v2[p]) for p in pairs]
    u2 = [_dot(inv[p].astype(BF16), x[p].astype(BF16)).astype(BF16) for p in pairs]
    y2 = [_dot_nt(r2b[p], sb[p]) + _dot(m_rb[p], u2[p]) + _dot(m_rk[p], v2[p]) for p in pairs]
    for p in pairs:
        e_end = jnp.exp(total[p] - cum[p])
        uv = jnp.concatenate([u2[p], v2[p]], axis=0)
        bk = jnp.concatenate([stack(b[p] * e_end), stack(k[p] * e_end)], axis=0).astype(BF16)
        state_ref[p] = state[p] * jnp.exp(total[p]) + _dot_tn(uv, bk)
    y = [x[0:c] + x[c:] for x in y2]
    mean = [_head_sum(x) * (1.0 / RWKV_HEAD) for x in y]
    d = [y[p] - mean[p] for p in pairs]
    var = [_head_sum(x * x) * (1.0 / RWKV_HEAD) for x in d]
    bonus = [_head_sum(r[p] * k[p] * rk_ref[:, sls[p]]) for p in pairs]
    for p in pairs:
        sl = sls[p]
        out = d[p] * lax.rsqrt(var[p] + GN_EPS) * lnw_ref[:, sl] + lnb_ref[:, sl]
        o_ref[:, sl] = (out + bonus[p] * v[p]) * g_ref[:, sl]


def _rwkv_scan(r, lw, k, v, kk, a, g, r_k, ln_w, ln_b):
    s, w = r.shape
    blk = pl.BlockSpec((SCAN_CHUNK, w), lambda c: (c, 0))
    row = pl.BlockSpec((1, w), lambda c: (0, 0))
    return pl.pallas_call(
        _scan_kernel, grid=(s // SCAN_CHUNK,),
        in_specs=[blk] * 7 + [row] * 3,
        out_specs=blk,
        out_shape=jax.ShapeDtypeStruct((s, w), F32),
        scratch_shapes=[pltpu.VMEM((w // LANES, LANES, LANES), F32)],
        compiler_params=_params(1), name="rwkv_scan")(r, lw, k, v, kk, a, g, r_k, ln_w, ln_b)


def _merge_kernel(oc_ref, os_ref, ow_ref, bg_ref, ob_ref, mg_ref, pa_ref, pb_ref, o_ref):
    bg = jax.nn.sigmoid(bg_ref[...])
    heads = []
    for h in range(NSA_HEADS):
        sl = slice(h * HEAD_DIM, (h + 1) * HEAD_DIM)
        c0 = NSA_BRANCHES * h
        heads.append(bg[:, c0:c0 + 1] * oc_ref[:, sl] + bg[:, c0 + 1:c0 + 2] * os_ref[:, sl]
                     + bg[:, c0 + 2:c0 + 3] * ow_ref[:, sl])
    o_a = jnp.concatenate(heads, axis=-1).astype(BF16)
    y_a = _dot(o_a, pa_ref[...])
    y_b = _dot(ob_ref[...].astype(BF16), pb_ref[...])
    mg = jax.nn.sigmoid(mg_ref[...].astype(F32))
    o_ref[...] = (mg[:, :D_MODEL] * y_a + mg[:, D_MODEL:] * y_b).astype(o_ref.dtype)


def _merge(o_cmp, o_slc, o_win, bgate, o_b, mgate, proj_a, proj_b, tm=256):
    s = o_cmp.shape[0]
    half = lambda: pl.BlockSpec((tm, NSA_WIDTH), lambda i: (i, 0))
    wspec = pl.BlockSpec((NSA_WIDTH, D_MODEL), lambda i: (0, 0))
    return pl.pallas_call(
        _merge_kernel, grid=(s // tm,),
        in_specs=[half(), half(), half(), pl.BlockSpec((tm, LANES), lambda i: (i, 0)), half(),
                  pl.BlockSpec((tm, 2 * D_MODEL), lambda i: (i, 0)), wspec, wspec],
        out_specs=pl.BlockSpec((tm, D_MODEL), lambda i: (i, 0)),
        out_shape=jax.ShapeDtypeStruct((s, D_MODEL), BF16),
        compiler_params=_params(1), name="merge")(o_cmp, o_slc, o_win, bgate, o_b, mgate, proj_a, proj_b)


def _outproj_kernel(x_ref, m_ref, w_ref, g_ref, x1_ref, h_ref):
    x1 = x_ref[...] + _dot(m_ref[...], w_ref[...])
    x1_ref[...] = x1
    y = x1 * lax.rsqrt(jnp.mean(x1 * x1, axis=-1, keepdims=True) + EPS)
    h_ref[...] = (y * g_ref[...]).astype(h_ref.dtype)


def _outproj(x, mixed, w_out, g, tm=256):
    s, d = x.shape
    blk = pl.BlockSpec((tm, d), lambda i: (i, 0))
    return pl.pallas_call(
        _outproj_kernel, grid=(s // tm,),
        in_specs=[blk, blk, pl.BlockSpec((d, d), lambda i: (0, 0)), pl.BlockSpec((1, d), lambda i: (0, 0))],
        out_specs=[blk, blk],
        out_shape=[jax.ShapeDtypeStruct((s, d), F32), jax.ShapeDtypeStruct((s, d), BF16)],
        compiler_params=_params(1), name="outproj")(x, mixed, w_out, g.reshape(1, d))


def _ffn_up_kernel(h_ref, wg_ref, wv_ref, cwg_ref, cwv_ref, cbg_ref, cbv_ref, o_ref, ug_ref, uv_ref):
    tm = h_ref.shape[0]

    @pl.when(pl.program_id(1) == 0)
    def _():
        ug_ref[0:8, :] = jnp.zeros((8, ug_ref.shape[1]), F32)
        uv_ref[0:8, :] = jnp.zeros((8, uv_ref.shape[1]), F32)

    h = h_ref[...]

    def conv(u_ref, w_ref, cw_ref, cb_ref):
        u_ref[8:8 + tm, :] = _dot(h, w_ref[...])
        out = cb_ref[...]
        for tap in range(CONV_WIDTH):
            off = 8 - (CONV_WIDTH - 1 - tap)
            out = out + u_ref[off:off + tm, :] * cw_ref[tap:tap + 1, :]
        u_ref[0:8, :] = u_ref[tm:tm + 8, :]
        return out

    gate = conv(ug_ref, wg_ref, cwg_ref, cbg_ref)
    val = conv(uv_ref, wv_ref, cwv_ref, cbv_ref)
    o_ref[...] = (jax.nn.silu(gate) * val).astype(o_ref.dtype)


def _ffn_up(h, w_up, conv_w, conv_b, tm=512, tn=512):
    s, d = h.shape
    nj = D_FF // tn
    return pl.pallas_call(
        _ffn_up_kernel, grid=(nj, s // tm),
        in_specs=[pl.BlockSpec((tm, d), lambda j, i: (i, 0)),
                  pl.BlockSpec((d, tn), lambda j, i: (0, j)),
                  pl.BlockSpec((d, tn), lambda j, i: (0, nj + j)),
                  pl.BlockSpec((CONV_WIDTH, tn), lambda j, i: (0, j)),
                  pl.BlockSpec((CONV_WIDTH, tn), lambda j, i: (0, nj + j)),
                  pl.BlockSpec((1, tn), lambda j, i: (0, j)),
                  pl.BlockSpec((1, tn), lambda j, i: (0, nj + j))],
        out_specs=pl.BlockSpec((tm, tn), lambda j, i: (i, j)),
        out_shape=jax.ShapeDtypeStruct((s, D_FF), BF16),
        scratch_shapes=[pltpu.VMEM((tm + 8, tn), F32), pltpu.VMEM((tm + 8, tn), F32)],
        compiler_params=_params(2), name="ffn_up")(h, w_up, w_up, conv_w, conv_w, conv_b, conv_b)


def _ffn_down_kernel(a_ref, w_ref, x_ref, g_ref, o_ref, y_ref):
    n = pl.program_id(1)
    y_ref[n] = x_ref[...] + _dot(a_ref[...], w_ref[...])

    @pl.when(n == pl.num_programs(1) - 1)
    def _():
        x2 = jnp.concatenate([y_ref[c] for c in range(y_ref.shape[0])], axis=-1)
        y = x2 * lax.rsqrt(jnp.mean(x2 * x2, axis=-1, keepdims=True) + EPS)
        o_ref[...] = y * g_ref[...]


def _ffn_down(act, w_down, x1, g, tm=512, tn=512):
    s, d = x1.shape
    return pl.pallas_call(
        _ffn_down_kernel, grid=(s // tm, d // tn),
        in_specs=[pl.BlockSpec((tm, D_FF), lambda i, n: (i, 0)),
                  pl.BlockSpec((D_FF, tn), lambda i, n: (0, n)),
                  pl.BlockSpec((tm, tn), lambda i, n: (i, n)),
                  pl.BlockSpec((1, d), lambda i, n: (0, 0))],
        out_specs=pl.BlockSpec((tm, d), lambda i, n: (i, 0)),
        out_shape=jax.ShapeDtypeStruct((s, d), F32),
        scratch_shapes=[pltpu.VMEM((d // tn, tm, tn), F32)],
        compiler_params=_params(2), name="ffn_down")(act, w_down, x1, g.reshape(1, d))


def _pad_cols(w, n):
    return jnp.pad(w, ((0, 0), (0, n - w.shape[1])))


def _pad_rows(w, n):
    return jnp.pad(w, ((0, n - w.shape[0]), (0, 0)))


def _split_rwkv_cols(t):
    w = RWKV_WIDTH
    rkv, wd, ad, gd = (t[:, :3 * w], t[:, 3 * w:3 * w + DECAY_LORA],
                       t[:, 3 * w + DECAY_LORA:3 * w + DECAY_LORA + ICLR_LORA],
                       t[:, 3 * w + DECAY_LORA + ICLR_LORA:])
    return jnp.concatenate([rkv, _pad_cols(wd, LORA_PAD), _pad_cols(ad, LORA_PAD), gd], axis=1)


def _rope_tables(seq):
    half = HEAD_DIM // 2
    inv = ROPE_THETA ** (-jnp.arange(half, dtype=F32) / half)
    ang = jnp.arange(seq, dtype=F32)[:, None] * inv[None, :]
    cos, sin = jnp.cos(ang), jnp.sin(ang)
    return jnp.concatenate([cos, cos], axis=1), jnp.concatenate([-sin, sin], axis=1)


def _overlap_matrix(nc):
    c_start = jnp.arange(nc) * CMP_STRIDE
    s_start = jnp.arange(LANES) * SEL_BLOCK
    ov = jnp.clip(jnp.minimum(c_start[None, :] + CMP_BLOCK, s_start[:, None] + SEL_BLOCK)
                  - jnp.maximum(c_start[None, :], s_start[:, None]), 0)
    return ov.astype(F32) / CMP_BLOCK


def _layer(x, norm1_g, w_in, cmp_k_pe, cmp_k_w1, cmp_k_w2, cmp_v_pe, cmp_v_w1, cmp_v_w2, rwkv_mu, rwkv_w0,
           rwkv_w_up, rwkv_a0, rwkv_a_up, rwkv_g_up, rwkv_k_k, rwkv_k_a, rwkv_r_k, rwkv_ln_w, rwkv_ln_b,
           proj_nsa, proj_rwkv, w_out, norm2_g, ffn_up, ffn_conv_w, ffn_conv_b, ffn_down, final_g):
    seq = x.shape[0]
    n_sel = seq // SEL_BLOCK
    assert seq % 2048 == 0 and n_sel <= LANES
    nc = seq // CMP_STRIDE
    w = RWKV_WIDTH

    c_gate = QKV_WIDTH
    c_z = c_gate + NSA_HEADS * NSA_BRANCHES
    c_merge = c_z + 3 * w + DECAY_LORA + ICLR_LORA + GATE_LORA
    w_qkv = w_in[:, :c_gate].astype(BF16)
    w_bgate = _pad_cols(w_in[:, c_gate:c_z], LANES).astype(BF16)
    w_z = _split_rwkv_cols(w_in[:, c_z:c_merge]).astype(BF16)
    w_merge = w_in[:, c_merge:].astype(BF16)

    h = _rmsnorm(x, norm1_g, BF16)
    qkv = _matmul(h, w_qkv, F32, 1024, 1280, "proj_qkv")
    bgate = _matmul(h, w_bgate, F32, 1024, LANES, "proj_bgate")
    z = _matmul(h, w_z, F32, 1024, 896, "proj_rwkv")
    mgate = _matmul(h, w_merge, BF16, 1024, 1024, "proj_merge")

    cosf, sinf = _rope_tables(seq)
    qkv = _rope(qkv, cosf, sinf)
    kv_c = qkv[:, NSA_WIDTH:NSA_WIDTH + 2 * KV_WIDTH]
    tok = kv_c.reshape(nc, CMP_STRIDE, 2, NSA_GROUPS, HEAD_DIM)
    tok = jnp.transpose(tok, (2, 3, 0, 1, 4)).reshape(2, NSA_GROUPS, nc, CMP_STRIDE * HEAD_DIM)
    tok_next = jnp.concatenate([tok[:, :, 1:], jnp.zeros_like(tok[:, :, :1])], axis=2)
    pe = jnp.stack([cmp_k_pe, cmp_v_pe]).reshape(2, 1, CMP_BLOCK * HEAD_DIM)
    kv_cmp = _compress(tok, tok_next, pe, jnp.stack([cmp_k_w1, cmp_v_w1]), jnp.stack([cmp_k_w2, cmp_v_w2]))
    o_cmp, sel = _cmp_attention(qkv, kv_cmp, _overlap_matrix(nc), min(SEL_TOPK, n_sel))
    o_slc = _slc_attention(qkv, sel)
    o_win = _win_attention(qkv)

    mu = _split_rwkv_cols(rwkv_mu.reshape(1, -1))
    r, lw, k, v, kk, a, g = _rwkv_prep(
        z, mu, rwkv_w0.reshape(1, w), _pad_rows(rwkv_w_up, LORA_PAD), rwkv_a0.reshape(1, w),
        _pad_rows(rwkv_a_up, LORA_PAD), rwkv_g_up, rwkv_k_k.reshape(1, w), rwkv_k_a.reshape(1, w))
    o_b = _rwkv_scan(r, lw, k, v, kk, a, g, rwkv_r_k.reshape(1, w), rwkv_ln_w.reshape(1, w),
                     rwkv_ln_b.reshape(1, w))

    mixed = _merge(o_cmp, o_slc, o_win, bgate, o_b, mgate, proj_nsa.astype(BF16), proj_rwkv.astype(BF16))
    x1, h2 = _outproj(x, mixed, w_out.astype(BF16), norm2_g)
    act = _ffn_up(h2, ffn_up.astype(BF16), ffn_conv_w, ffn_conv_b.reshape(1, -1))
    return _ffn_down(act, ffn_down.astype(BF16), x1, final_g)


def kernel(x, norm1_g, w_in, cmp_k_pe, cmp_k_w1, cmp_k_w2, cmp_v_pe, cmp_v_w1, cmp_v_w2, rwkv_mu, rwkv_w0,
           rwkv_w_up, rwkv_a0, rwkv_a_up, rwkv_g_up, rwkv_k_k, rwkv_k_a, rwkv_r_k, rwkv_ln_w, rwkv_ln_b,
           proj_nsa, proj_rwkv, w_out, norm2_g, ffn_up, ffn_conv_w, ffn_conv_b, ffn_down, final_g):
    assert norm1_g.shape[0] == 1, "single-layer block: the final rmsnorm is fused into the down projection"
    layer = (norm1_g, w_in, cmp_k_pe, cmp_k_w1, cmp_k_w2, cmp_v_pe, cmp_v_w1, cmp_v_w2, rwkv_mu, rwkv_w0,
             rwkv_w_up, rwkv_a0, rwkv_a_up, rwkv_g_up, rwkv_k_k, rwkv_k_a, rwkv_r_k, rwkv_ln_w, rwkv_ln_b,
             proj_nsa, proj_rwkv, w_out, norm2_g, ffn_up, ffn_conv_w, ffn_conv_b, ffn_down)
    weights = [t.reshape(t.shape[1:]) for t in layer]
    batch, seq, d = x.shape
    if batch == 1:
        return _layer(x.reshape(seq, d), *weights, final_g).reshape(1, seq, d)
    return jnp.stack([_layer(x[b], *weights, final_g) for b in range(batch)])
```
